```python
import math
import jax, jax.numpy as jnp
from jax import lax
import numpy as np

D_MODEL = 1024
BATCH = 16
SEQ = 4096
DEPTH = 4

N_MIXERS = 2
N_META = 16
Q_BLOCK = 128
META_PAD = Q_BLOCK - N_META
EPS = 1e-6
NEG_BIG = -1e30

CONV_EXPAND = 2
CONV_WIDTH = CONV_EXPAND * D_MODEL
CONV_KERNEL = 31
CONV_IN = 3 * CONV_WIDTH

DIFF_HEADS = 8
DIFF_HEAD_DIM = D_MODEL // DIFF_HEADS // 2
DIFF_V_DIM = 2 * DIFF_HEAD_DIM
DIFF_QK_WIDTH = 2 * DIFF_HEADS * DIFF_HEAD_DIM
DIFF_WIDTH = DIFF_HEADS * DIFF_V_DIM
DIFF_IN = 2 * DIFF_QK_WIDTH + 2 * DIFF_WIDTH

N_CONV_LAYERS = (DEPTH + 1) // 2
N_ATTN_LAYERS = DEPTH // 2

kernel_name = "interleaved_conformer_conv_diff_attention_trunk"


def rms_norm(x, gain):
    xf = x.astype(jnp.float32)
    y = xf * lax.rsqrt(jnp.mean(xf * xf, axis=-1, keepdims=True) + EPS)
    return (y * gain.astype(jnp.float32)).astype(x.dtype)


def layer_norm(x, gain, bias):
    xf = x.astype(jnp.float32)
    mu = jnp.mean(xf, axis=-1, keepdims=True)
    xc = xf - mu
    var = jnp.mean(xc * xc, axis=-1, keepdims=True)
    y = xc * lax.rsqrt(var + EPS) * gain.astype(jnp.float32) + bias.astype(jnp.float32)
    return y.astype(x.dtype)


def conv_mixer(h, w_in, dw_kernel, dw_bias, ln_gain, ln_bias, w_out):
    proj = jnp.einsum("bld,dc->blc", h, w_in)
    u, g, z = jnp.split(proj, 3, axis=-1)
    v = u * jax.nn.sigmoid(g)
    v = lax.conv_general_dilated(
        v, dw_kernel[:, None, :].astype(v.dtype),
        window_strides=(1,), padding=[(CONV_KERNEL - 1, 0)],
        dimension_numbers=("NWC", "WIO", "NWC"),
        feature_group_count=CONV_WIDTH) + dw_bias
    v = jax.nn.silu(layer_norm(v, ln_gain, ln_bias))
    return jnp.einsum("ble,ed->bld", v * jax.nn.silu(z), w_out)


def diff_attention_mixer(h, w_in, lam_q1, lam_k1, lam_q2, lam_k2, subln_gain, w_out, lambda_init):
    b, l, _ = h.shape
    proj = jnp.einsum("bld,dc->blc", h, w_in)
    q, k, v, z = jnp.split(proj, [DIFF_QK_WIDTH, 2 * DIFF_QK_WIDTH,
                                  2 * DIFF_QK_WIDTH + DIFF_WIDTH], axis=-1)
    q = q.reshape(b, l, DIFF_HEADS, 2, DIFF_HEAD_DIM)
    k = k.reshape(b, l, DIFF_HEADS, 2, DIFF_HEAD_DIM)
    v = v.reshape(b, l, DIFF_HEADS, DIFF_V_DIM)
    pad5 = ((0, 0), (META_PAD, 0), (0, 0), (0, 0), (0, 0))
    q = jnp.pad(q, pad5)
    k = jnp.pad(k, pad5)
    v = jnp.pad(v, ((0, 0), (META_PAD, 0), (0, 0), (0, 0)))
    l_pad = l + META_PAD
    n_blocks = l_pad // Q_BLOCK
    pos = jnp.arange(l_pad, dtype=jnp.int32) - META_PAD

    lam = (jnp.exp(jnp.sum(lam_q1.astype(jnp.float32) * lam_k1.astype(jnp.float32)))
           - jnp.exp(jnp.sum(lam_q2.astype(jnp.float32) * lam_k2.astype(jnp.float32)))
           + lambda_init)
    slopes = jnp.exp2(-8.0 * jnp.arange(1, DIFF_HEADS + 1, dtype=jnp.float32) / DIFF_HEADS)
    scale = DIFF_HEAD_DIM ** -0.5

    outs = []
    for i in range(n_blocks):
        qs, qe = i * Q_BLOCK, (i + 1) * Q_BLOCK
        qb = q[:, qs:qe]
        kb = k[:, :qe]
        vb = v[:, :qe]
        s = jnp.einsum("bqhmd,bkhmd->bhmqk", qb, kb,
                       preferred_element_type=jnp.float32) * scale
        qpos = pos[qs:qe]
        kpos = pos[:qe]
        dist = (qpos[:, None] - kpos[None, :]).astype(jnp.float32)
        alibi = -slopes[:, None, None] * jnp.abs(dist)[None]
        valid = (kpos[None, :] <= qpos[:, None]) & (kpos[None, :] >= 0)
        s = jnp.where(valid, s + alibi[:, None], NEG_BIG)
        p = jax.nn.softmax(s, axis=-1)
        a = p[:, :, 0] - lam * p[:, :, 1]
        outs.append(jnp.einsum("bhqk,bkhe->bqhe", a.astype(vb.dtype), vb))
    o = jnp.concatenate(outs, axis=1)[:, META_PAD:]
    o = rms_norm(o, subln_gain) * (1.0 - lambda_init)
    o = o.reshape(b, l, DIFF_WIDTH) * jax.nn.silu(z)
    return jnp.einsum("blc,cd->bld", o, w_out)


def setup_inputs(seed: int = 0) -> dict:
    key = jax.random.key(seed)
    ks = jax.random.split(key, 20)
    f32 = jnp.float32
    nrm = lambda k, shape, s: jax.random.normal(k, shape, f32) * s
    return {
        "x": nrm(ks[0], (BATCH, SEQ, D_MODEL), 1.0),
        "meta_tokens": nrm(ks[1], (N_META, D_MODEL), 1.0),
        "norm_gain": 1.0 + nrm(ks[2], (DEPTH, D_MODEL), 0.02),
        "final_norm_gain": 1.0 + nrm(ks[3], (D_MODEL,), 0.02),
        "conv_w_in": nrm(ks[4], (N_CONV_LAYERS, D_MODEL, CONV_IN), D_MODEL ** -0.5),
        "conv_dw_kernel": nrm(ks[5], (N_CONV_LAYERS, CONV_KERNEL, CONV_WIDTH), CONV_KERNEL ** -0.5),
        "conv_dw_bias": nrm(ks[6], (N_CONV_LAYERS, CONV_WIDTH), 0.01),
        "conv_ln_gain": 1.0 + nrm(ks[7], (N_CONV_LAYERS, CONV_WIDTH), 0.02),
        "conv_ln_bias": nrm(ks[8], (N_CONV_LAYERS, CONV_WIDTH), 0.01),
        "conv_w_out": nrm(ks[9], (N_CONV_LAYERS, CONV_WIDTH, D_MODEL), CONV_WIDTH ** -0.5),
        "attn_w_in": nrm(ks[10], (N_ATTN_LAYERS, D_MODEL, DIFF_IN), D_MODEL ** -0.5),
        "attn_lambda_q1": nrm(ks[11], (N_ATTN_LAYERS, DIFF_HEAD_DIM), 0.1),
        "attn_lambda_k1": nrm(ks[12], (N_ATTN_LAYERS, DIFF_HEAD_DIM), 0.1),
        "attn_lambda_q2": nrm(ks[13], (N_ATTN_LAYERS, DIFF_HEAD_DIM), 0.1),
        "attn_lambda_k2": nrm(ks[14], (N_ATTN_LAYERS, DIFF_HEAD_DIM), 0.1),
        "attn_subln_gain": 1.0 + nrm(ks[15], (N_ATTN_LAYERS, DIFF_V_DIM), 0.02),
        "attn_w_out": nrm(ks[16], (N_ATTN_LAYERS, DIFF_WIDTH, D_MODEL), DIFF_WIDTH ** -0.5),
    }


def reference(x, meta_tokens, norm_gain, final_norm_gain,
              conv_w_in, conv_dw_kernel, conv_dw_bias, conv_ln_gain, conv_ln_bias, conv_w_out,
              attn_w_in, attn_lambda_q1, attn_lambda_k1, attn_lambda_q2, attn_lambda_k2,
              attn_subln_gain, attn_w_out):
    b = x.shape[0]
    meta = jnp.broadcast_to(meta_tokens[None].astype(x.dtype), (b, N_META, D_MODEL))
    h = jnp.concatenate([meta, x], axis=1)
    for i in range(DEPTH):
        hn = rms_norm(h, norm_gain[i])
        j = i // N_MIXERS
        if i % N_MIXERS == 0:
            y = conv_mixer(hn, conv_w_in[j], conv_dw_kernel[j], conv_dw_bias[j],
                           conv_ln_gain[j], conv_ln_bias[j], conv_w_out[j])
        else:
            lambda_init = 0.8 - 0.6 * math.exp(-0.3 * i)
            y = diff_attention_mixer(hn, attn_w_in[j], attn_lambda_q1[j], attn_lambda_k1[j],
                                     attn_lambda_q2[j], attn_lambda_k2[j], attn_subln_gain[j],
                                     attn_w_out[j], lambda_init)
        h = h + y
    return rms_norm(h, final_norm_gain)[:, N_META:]
```

```python
import functools
import math

import jax
import jax.numpy as jnp
from jax import lax
from jax.experimental import pallas as pl
from jax.experimental.pallas import tpu as pltpu

F32 = jnp.float32
BF16 = jnp.bfloat16

EPS = 1e-6
NEG_BIG = -1e30
N_META = 16
TILE = 256
FRONT = TILE - N_META
CONV_KERNEL = 31
CARRY = 32
CONV_ROWS = 128
LANES = 128
HEADS = 8
HEAD_DIM = 64
V_DIM = 2 * HEAD_DIM
VMEM_LIMIT = 56 * 1024 * 1024


def _rms(x, gain):
    return x * lax.rsqrt(jnp.mean(x * x, axis=-1, keepdims=True) + EPS) * gain


def _silu(x):
    return x * jax.nn.sigmoid(x)


def _const_spec(shape):
    return pl.BlockSpec(shape, lambda b, i: (0,) * len(shape), pipeline_mode=pl.Buffered(1))


def _conv_layer_kernel(h_ref, ng_ref, win_ref, dwk_ref, dwb_ref, lng_ref, lnb_ref, wout_ref,
                       o_ref, vbuf_ref, cbuf_ref):
    i = pl.program_id(1)
    width = cbuf_ref.shape[1]
    h = h_ref[0]
    hn = _rms(h, ng_ref[...]).astype(BF16)
    u = jnp.dot(hn, win_ref[:, 0:width], preferred_element_type=F32)
    g = jnp.dot(hn, win_ref[:, width:2 * width], preferred_element_type=F32)
    v = u * jax.nn.sigmoid(g)
    row = lax.broadcasted_iota(jnp.int32, (TILE, 1), 0)
    v = jnp.where((i > 0) | (row >= FRONT), v, 0.0)

    @pl.when(i == 0)
    def _():
        vbuf_ref[0:CARRY, :] = jnp.zeros((CARRY, width), F32)

    vbuf_ref[CARRY:, :] = v

    base = CARRY - (CONV_KERNEL - 1)

    def col_body(c, carry):
        cols = pl.ds(pl.multiple_of(c * LANES, LANES), LANES)
        for r in range(TILE // CONV_ROWS):
            acc = jnp.broadcast_to(dwb_ref[:, cols], (CONV_ROWS, LANES))
            for j in range(CONV_KERNEL):
                start = r * CONV_ROWS + base + j
                acc = acc + dwk_ref[j:j + 1, cols] * vbuf_ref[start:start + CONV_ROWS, cols]
            cbuf_ref[r * CONV_ROWS:(r + 1) * CONV_ROWS, cols] = acc
        return carry

    lax.fori_loop(0, width // LANES, col_body, 0)
    vbuf_ref[0:CARRY, :] = vbuf_ref[TILE:TILE + CARRY, :]

    c = cbuf_ref[...]
    mu = jnp.mean(c, axis=-1, keepdims=True)
    xc = c - mu
    var = jnp.mean(xc * xc, axis=-1, keepdims=True)
    y = _silu(xc * lax.rsqrt(var + EPS) * lng_ref[...] + lnb_ref[...])
    z = jnp.dot(hn, win_ref[:, 2 * width:3 * width], preferred_element_type=F32)
    y = (y * _silu(z)).astype(BF16)
    o_ref[0] = h + jnp.dot(y, wout_ref[...], preferred_element_type=F32)


def _conv_layer(hp, norm_gain, w_in, dw_kernel, dw_bias, ln_gain, ln_bias, w_out):
    b, lp, d = hp.shape
    width = w_out.shape[0]
    nt = lp // TILE
    tile_spec = pl.BlockSpec((1, TILE, d), lambda bb, i: (bb, i, 0))
    return pl.pallas_call(
        _conv_layer_kernel,
        grid=(b, nt),
        in_specs=[
            tile_spec,
            _const_spec((1, d)),
            _const_spec((d, 3 * width)),
            _const_spec((CONV_KERNEL, width)),
            _const_spec((1, width)),
            _const_spec((1, width)),
            _const_spec((1, width)),
            _const_spec((width, d)),
        ],
        out_specs=tile_spec,
        out_shape=jax.ShapeDtypeStruct((b, lp, d), F32),
        scratch_shapes=[
            pltpu.VMEM((CARRY + TILE, width), F32),
            pltpu.VMEM((TILE, width), F32),
        ],
        compiler_params=pltpu.CompilerParams(
            dimension_semantics=("arbitrary", "arbitrary"), vmem_limit_bytes=VMEM_LIMIT),
        name="conv_layer",
    )(hp, norm_gain.reshape(1, d), w_in.astype(BF16), dw_kernel, dw_bias.reshape(1, width),
      ln_gain.reshape(1, width), ln_bias.reshape(1, width), w_out.astype(BF16))


def _attn_layer_kernel(h_ref, ng_ref, win_ref, lq1_ref, lk1_ref, lq2_ref, lk2_ref, sg_ref,
                       wout_ref, fg_ref, o_ref, kc_ref, vc_ref, q_ref, zs_ref, og_ref,
                       m_ref, l_ref, acc_ref, *, lambda_init, final):
    i = pl.program_id(1)
    d = h_ref.shape[2]
    qk_width = HEADS * 2 * HEAD_DIM
    v_width = HEADS * V_DIM
    h = h_ref[0]
    hn = _rms(h, ng_ref[...]).astype(BF16)
    q = jnp.dot(hn, win_ref[:, 0:qk_width], preferred_element_type=F32)
    q_ref[...] = (q * (HEAD_DIM ** -0.5)).astype(BF16)
    k = jnp.dot(hn, win_ref[:, qk_width:2 * qk_width], preferred_element_type=F32)
    kc_ref[i] = k.astype(BF16)
    v = jnp.dot(hn, win_ref[:, 2 * qk_width:2 * qk_width + v_width], preferred_element_type=F32)
    vc_ref[i] = v.astype(BF16)
    z = jnp.dot(hn, win_ref[:, 2 * qk_width + v_width:], preferred_element_type=F32)
    zs_ref[...] = _silu(z)

    lam = (jnp.exp(jnp.sum(lq1_ref[...] * lk1_ref[...], axis=-1, keepdims=True))
           - jnp.exp(jnp.sum(lq2_ref[...] * lk2_ref[...], axis=-1, keepdims=True))
           + lambda_init)

    lane = lax.broadcasted_iota(jnp.int32, (1, V_DIM), 1)
    qrow = lax.broadcasted_iota(jnp.int32, (2 * TILE, TILE), 0)
    qrow = jnp.where(qrow >= TILE, qrow - TILE, qrow)
    kcol = lax.broadcasted_iota(jnp.int32, (2 * TILE, TILE), 1)
    rel = qrow - kcol

    for hd in range(HEADS):
        cols = slice(hd * V_DIM, (hd + 1) * V_DIM)
        slope = 2.0 ** (-8.0 * (hd + 1) / HEADS)
        qh = q_ref[:, cols]
        q2 = jnp.concatenate([jnp.where(lane < HEAD_DIM, qh, jnp.zeros_like(qh)),
                              jnp.where(lane >= HEAD_DIM, qh, jnp.zeros_like(qh))], axis=0)
        m_ref[...] = jnp.full(m_ref.shape, NEG_BIG, F32)
        l_ref[...] = jnp.zeros(l_ref.shape, F32)
        acc_ref[...] = jnp.zeros(acc_ref.shape, F32)

        def kv_body(j, carry):
            kj = kc_ref[j, :, cols]
            vj = vc_ref[j, :, cols]
            s = lax.dot_general(q2, kj, (((1,), (1,)), ((), ())), preferred_element_type=F32)
            dist = rel + (i - j) * TILE
            valid = (dist >= 0) & ((j > 0) | (kcol >= FRONT))
            s = jnp.where(valid, s - slope * dist.astype(F32), NEG_BIG)
            m_prev = m_ref[...]
            m_new = jnp.maximum(m_prev, jnp.max(s, axis=-1, keepdims=True))
            alpha = jnp.exp(m_prev - m_new)
            p = jnp.exp(s - m_new)
            l_ref[...] = alpha * l_ref[...] + jnp.sum(p, axis=-1, keepdims=True)
            acc_ref[...] = alpha * acc_ref[...] + jnp.dot(p.astype(BF16), vj,
                                                          preferred_element_type=F32)
            m_ref[...] = m_new
            return carry

        lax.fori_loop(0, i + 1, kv_body, 0)
        o = (acc_ref[0:TILE, :] / l_ref[0:TILE, :]
             - lam * (acc_ref[TILE:, :] / l_ref[TILE:, :]))
        o = _rms(o, sg_ref[...]) * (1.0 - lambda_init)
        og_ref[:, cols] = (o * zs_ref[:, cols]).astype(BF16)

    y = h + jnp.dot(og_ref[...], wout_ref[...], preferred_element_type=F32)
    if final:
        y = _rms(y, fg_ref[...])
    o_ref[0] = y


def _attn_layer(hp, norm_gain, w_in, lq1, lk1, lq2, lk2, subln_gain, w_out, final_gain,
                lambda_init, final):
    b, lp, d = hp.shape
    nt = lp // TILE
    v_width = w_out.shape[0]
    tile_spec = pl.BlockSpec((1, TILE, d), lambda bb, i: (bb, i, 0))
    if final:
        out_spec = pl.BlockSpec((1, TILE, d), lambda bb, i: (bb, jnp.maximum(i - 1, 0), 0))
        out_shape = jax.ShapeDtypeStruct((b, lp - TILE, d), F32)
    else:
        out_spec = tile_spec
        out_shape = jax.ShapeDtypeStruct((b, lp, d), F32)
    kern = functools.partial(_attn_layer_kernel, lambda_init=lambda_init, final=final)
    return pl.pallas_call(
        kern,
        grid=(b, nt),
        in_specs=[
            tile_spec,
            _const_spec((1, d)),
            _const_spec(w_in.shape),
            _const_spec((1, HEAD_DIM)),
            _const_spec((1, HEAD_DIM)),
            _const_spec((1, HEAD_DIM)),
            _const_spec((1, HEAD_DIM)),
            _const_spec((1, V_DIM)),
            _const_spec(w_out.shape),
            _const_spec((1, d)),
        ],
        out_specs=out_spec,
        out_shape=out_shape,
        scratch_shapes=[
            pltpu.VMEM((nt, TILE, HEADS * 2 * HEAD_DIM), BF16),
            pltpu.VMEM((nt, TILE, v_width), BF16),
            pltpu.VMEM((TILE, HEADS * 2 * HEAD_DIM), BF16),
            pltpu.VMEM((TILE, v_width), F32),
            pltpu.VMEM((TILE, v_width), BF16),
            pltpu.VMEM((2 * TILE, 1), F32),
            pltpu.VMEM((2 * TILE, 1), F32),
            pltpu.VMEM((2 * TILE, V_DIM), F32),
        ],
        compiler_params=pltpu.CompilerParams(
            dimension_semantics=("arbitrary", "arbitrary"), vmem_limit_bytes=VMEM_LIMIT),
        name="attn_layer_final" if final else "attn_layer",
    )(hp, norm_gain.reshape(1, d), w_in.astype(BF16), lq1.reshape(1, HEAD_DIM),
      lk1.reshape(1, HEAD_DIM), lq2.reshape(1, HEAD_DIM), lk2.reshape(1, HEAD_DIM),
      subln_gain.reshape(1, V_DIM), w_out.astype(BF16), final_gain.reshape(1, d))


def kernel(x, meta_tokens, norm_gain, final_norm_gain, conv_w_in, conv_dw_kernel, conv_dw_bias, conv_ln_gain, conv_ln_bias, conv_w_out, attn_w_in, attn_lambda_q1, attn_lambda_k1, attn_lambda_q2, attn_lambda_k2, attn_subln_gain, attn_w_out):
    b, seq, d = x.shape
    assert seq % TILE == 0 and meta_tokens.shape[0] == N_META
    depth = norm_gain.shape[0]
    assert depth % 2 == 0, "the last layer must be an attention layer (it applies the final norm)"
    meta = jnp.broadcast_to(meta_tokens[None].astype(x.dtype), (b, N_META, d))
    hp = jnp.concatenate([jnp.zeros((b, FRONT, d), x.dtype), meta, x], axis=1)
    for layer in range(depth):
        j = layer // 2
        if layer % 2 == 0:
            hp = _conv_layer(hp, norm_gain[layer], conv_w_in[j], conv_dw_kernel[j], conv_dw_bias[j],
                             conv_ln_gain[j], conv_ln_bias[j], conv_w_out[j])
        else:
            lambda_init = 0.8 - 0.6 * math.exp(-0.3 * layer)
            hp = _attn_layer(hp, norm_gain[layer], attn_w_in[j], attn_lambda_q1[j],
                             attn_lambda_k1[j], attn_lambda_q2[j], attn_lambda_k2[j],
                             attn_subln_gain[j], attn_w_out[j], final_norm_gain,
                             lambda_init, final=(layer == depth - 1))
    return hp
```

```python
import functools
import math
import struct

import jax
import jax.numpy as jnp
from jax import lax
from jax.experimental import pallas as pl
from jax.experimental.pallas import tpu as pltpu

F32 = jnp.float32
BF16 = jnp.bfloat16

EPS = 1e-6
NEG_BIG = -1e30
LOG2E = math.log2(math.e)
N_META = 16
TILE = 256
FRONT = TILE - N_META
CONV_KERNEL = 31
SUBLANES = 8
LANES = 128
CARRY = 32
CONV_ROWS = 128
HEADS = 8
HEAD_DIM = 64
V_DIM = 2 * HEAD_DIM
ONES_ROWS = 16
VMEM_LIMIT = 56 * 1024 * 1024
NT_DIMS = (((1,), (1,)), ((), ()))


def _rms(x, gain):
    return x * lax.rsqrt(jnp.mean(x * x, axis=-1, keepdims=True) + EPS) * gain


def _silu(x):
    return x * jax.nn.sigmoid(x)


def _const_spec(shape):
    return pl.BlockSpec(shape, lambda b, i: (0,) * len(shape), pipeline_mode=pl.Buffered(1))


def _conv_layer_kernel(h_ref, ng_ref, win_ref, dwk_ref, dwb_ref, lng_ref, lnb_ref, wout_ref,
                       o_ref, vbuf_ref, cbuf_ref):
    i = pl.program_id(1)
    width = cbuf_ref.shape[1]
    h = h_ref[0]
    hn = _rms(h, ng_ref[...]).astype(BF16)
    u = jnp.dot(hn, win_ref[:, 0:width], preferred_element_type=F32)
    g = jnp.dot(hn, win_ref[:, width:2 * width], preferred_element_type=F32)
    v = u * jax.nn.sigmoid(g)
    row = lax.broadcasted_iota(jnp.int32, (TILE, 1), 0)
    v = jnp.where((i > 0) | (row >= FRONT), v, 0.0)

    @pl.when(i == 0)
    def _():
        vbuf_ref[0:CARRY, :] = jnp.zeros((CARRY, width), F32)
        vbuf_ref[CARRY + TILE:, :] = jnp.zeros((SUBLANES, width), F32)

    vbuf_ref[CARRY:CARRY + TILE, :] = v

    base = CARRY - (CONV_KERNEL - 1)

    def col_body(c, carry):
        cols = pl.ds(pl.multiple_of(c * LANES, LANES), LANES)
        for r in range(TILE // CONV_ROWS):
            acc = jnp.broadcast_to(dwb_ref[:, cols], (CONV_ROWS, LANES))
            for res in range(SUBLANES):
                part = None
                for j in range(CONV_KERNEL):
                    if (base + j) % SUBLANES != res:
                        continue
                    start = r * CONV_ROWS + (base + j) - res
                    term = dwk_ref[j:j + 1, cols] * vbuf_ref[start:start + CONV_ROWS + SUBLANES, cols]
                    part = term if part is None else part + term
                acc = acc + part[res:res + CONV_ROWS, :]
            cbuf_ref[r * CONV_ROWS:(r + 1) * CONV_ROWS, cols] = acc
        return carry

    lax.fori_loop(0, width // LANES, col_body, 0)
    vbuf_ref[0:CARRY, :] = vbuf_ref[TILE:TILE + CARRY, :]

    c = cbuf_ref[...]
    mu = jnp.mean(c, axis=-1, keepdims=True)
    xc = c - mu
    var = jnp.mean(xc * xc, axis=-1, keepdims=True)
    y = _silu(xc * lax.rsqrt(var + EPS) * lng_ref[...] + lnb_ref[...])
    z = jnp.dot(hn, win_ref[:, 2 * width:3 * width], preferred_element_type=F32)
    y = (y * _silu(z)).astype(BF16)
    o_ref[0] = h + jnp.dot(y, wout_ref[...], preferred_element_type=F32)


def _conv_layer(hp, norm_gain, w_in, dw_kernel, dw_bias, ln_gain, ln_bias, w_out):
    b, lp, d = hp.shape
    width = w_out.shape[0]
    nt = lp // TILE
    tile_spec = pl.BlockSpec((1, TILE, d), lambda bb, i: (bb, i, 0))
    return pl.pallas_call(
        _conv_layer_kernel,
        grid=(b, nt),
        in_specs=[
            tile_spec,
            _const_spec((1, d)),
            _const_spec((d, 3 * width)),
            _const_spec((CONV_KERNEL, width)),
            _const_spec((1, width)),
            _const_spec((1, width)),
            _const_spec((1, width)),
            _const_spec((width, d)),
        ],
        out_specs=tile_spec,
        out_shape=jax.ShapeDtypeStruct((b, lp, d), F32),
        scratch_shapes=[
            pltpu.VMEM((CARRY + TILE + SUBLANES, width), F32),
            pltpu.VMEM((TILE, width), F32),
        ],
        compiler_params=pltpu.CompilerParams(
            dimension_semantics=("arbitrary", "arbitrary"), vmem_limit_bytes=VMEM_LIMIT),
        name="conv_layer",
    )(hp, norm_gain.reshape(1, d), w_in.astype(BF16), dw_kernel, dw_bias.reshape(1, width),
      ln_gain.reshape(1, width), ln_bias.reshape(1, width), w_out.astype(BF16))


def _bf16_round(x):
    bits = struct.unpack("<I", struct.pack("<f", x))[0]
    bits = (bits + 0x7FFF + ((bits >> 16) & 1)) & 0xFFFF0000
    return struct.unpack("<f", struct.pack("<I", bits))[0]


def _split_bf16(x, parts=3):
    out = []
    for _ in range(parts):
        out.append(_bf16_round(x))
        x -= out[-1]
    return out


LOG2E_PARTS = _split_bf16(LOG2E)


def _head_slope(hd):
    return sum(LOG2E_PARTS) * 2.0 ** (-8.0 * (hd + 1) / HEADS)


def _attend_tile(j, i, masked, kc_ref, vt_ref, q2_ref, kaug_ref, m_ref, acc_ref):
    if masked:
        krow = lax.broadcasted_iota(jnp.int32, (TILE, 2 * TILE), 0)
        qcol = lax.broadcasted_iota(jnp.int32, (TILE, 2 * TILE), 1)
        qcol = jnp.where(qcol >= TILE, qcol - TILE, qcol)
        valid = (qcol - krow + (i - j) * TILE >= 0) & (krow + j * TILE >= FRONT)
    tile_gap = ((j - i) * TILE).astype(F32)
    kaug = kaug_ref[...]
    ones = jnp.ones((ONES_ROWS, TILE), BF16)

    def scores(hd):
        kj = kc_ref[j, :, hd * V_DIM:(hd + 1) * V_DIM]
        return lax.dot_general(jnp.concatenate([kj, kaug], axis=1), q2_ref[hd], NT_DIMS,
                               preferred_element_type=F32)

    t_next = scores(0)
    for hd in range(HEADS):
        t = t_next
        if hd + 1 < HEADS:
            t_next = scores(hd + 1)
        if masked:
            t = jnp.where(valid, t, NEG_BIG)
        gap = _head_slope(hd) * tile_gap
        m_prev = m_ref[hd]
        m_new = jnp.maximum(m_prev, jnp.max(t, axis=0, keepdims=True) + gap)
        p = jnp.exp2(t - (m_new - gap)).astype(BF16)
        alpha = jnp.exp2(m_prev - m_new)
        vt = jnp.concatenate([vt_ref[j, hd * V_DIM:(hd + 1) * V_DIM, :], ones], axis=0)
        pv = jnp.dot(vt, p, preferred_element_type=F32)
        acc_ref[hd] = alpha * acc_ref[hd] + pv
        m_ref[hd] = m_new


def _attn_layer_kernel(h_ref, ng_ref, wq_ref, wk_ref, wvt_ref, wz_ref, lq1_ref, lk1_ref, lq2_ref,
                       lk2_ref, sg_ref, wout_ref, fg_ref, o_ref, kc_ref, vt_ref, q2_ref, kaug_ref,
                       zs_ref, ogt_ref, m_ref, acc_ref, *, lambda_init, final):
    i = pl.program_id(1)
    h = h_ref[0]
    hn = _rms(h, ng_ref[...]).astype(BF16)
    q = jnp.dot(hn, wq_ref[...], preferred_element_type=F32) * (LOG2E * HEAD_DIM ** -0.5)
    q = q.astype(BF16)
    lane = lax.broadcasted_iota(jnp.int32, (1, V_DIM), 1)
    krow = lax.broadcasted_iota(jnp.int32, (TILE, LANES), 0).astype(F32)
    lane_k = lax.broadcasted_iota(jnp.int32, (TILE, LANES), 1)
    kaug_ref[...] = jnp.where(lane_k < len(LOG2E_PARTS), krow, 0.0).astype(BF16)
    for hd in range(HEADS):
        qh = q[:, hd * V_DIM:(hd + 1) * V_DIM]
        q2_ref[hd, 0:TILE, 0:V_DIM] = jnp.where(lane < HEAD_DIM, qh, jnp.zeros_like(qh))
        q2_ref[hd, TILE:, 0:V_DIM] = jnp.where(lane >= HEAD_DIM, qh, jnp.zeros_like(qh))
        slope = jnp.zeros((1, LANES), F32)
        for n, part in enumerate(LOG2E_PARTS):
            slope = jnp.where(lane == n, part * 2.0 ** (-8.0 * (hd + 1) / HEADS), slope)
        q2_ref[hd, :, V_DIM:] = jnp.broadcast_to(slope, (2 * TILE, LANES)).astype(BF16)
    kc_ref[i] = jnp.dot(hn, wk_ref[...], preferred_element_type=F32).astype(BF16)
    vt_ref[i] = lax.dot_general(wvt_ref[...], hn, NT_DIMS,
                                preferred_element_type=F32).astype(BF16)
    zs_ref[...] = _silu(jnp.dot(hn, wz_ref[...], preferred_element_type=F32))

    m_ref[...] = jnp.full(m_ref.shape, NEG_BIG, F32)
    acc_ref[...] = jnp.zeros(acc_ref.shape, F32)
    state = (kc_ref, vt_ref, q2_ref, kaug_ref, m_ref, acc_ref)

    def masked_step(t, carry):
        _attend_tile(t * i, i, True, *state)
        return carry

    def interior_step(j, carry):
        _attend_tile(j, i, False, *state)
        return carry

    lax.fori_loop(0, jnp.where(i > 0, 2, 1), masked_step, 0)
    lax.fori_loop(1, i, interior_step, 0)

    lam = (jnp.exp(jnp.sum(lq1_ref[...] * lk1_ref[...], axis=-1, keepdims=True))
           - jnp.exp(jnp.sum(lq2_ref[...] * lk2_ref[...], axis=-1, keepdims=True))
           + lambda_init)
    for hd in range(HEADS):
        acc = acc_ref[hd, 0:V_DIM, :]
        l = acc_ref[hd, V_DIM:V_DIM + 1, :]
        o_t = acc[:, 0:TILE] / l[:, 0:TILE] - lam * (acc[:, TILE:] / l[:, TILE:])
        ogt_ref[hd * V_DIM:(hd + 1) * V_DIM, :] = o_t * lax.rsqrt(
            jnp.mean(o_t * o_t, axis=0, keepdims=True) + EPS)
    og = ogt_ref[...].T * sg_ref[...] * (1.0 - lambda_init) * zs_ref[...]
    y = h + jnp.dot(og.astype(BF16), wout_ref[...], preferred_element_type=F32)
    if final:
        y = _rms(y, fg_ref[...])
    o_ref[0] = y


def _attn_layer(hp, norm_gain, w_in, lq1, lk1, lq2, lk2, subln_gain, w_out, final_gain,
                lambda_init, final):
    b, lp, d = hp.shape
    nt = lp // TILE
    qk_width = HEADS * 2 * HEAD_DIM
    v_width = HEADS * V_DIM
    w_in = w_in.astype(BF16)
    wq = w_in[:, 0:qk_width]
    wk = w_in[:, qk_width:2 * qk_width]
    wvt = w_in[:, 2 * qk_width:2 * qk_width + v_width].T
    wz = w_in[:, 2 * qk_width + v_width:]
    tile_spec = pl.BlockSpec((1, TILE, d), lambda bb, i: (bb, i, 0))
    if final:
        out_spec = pl.BlockSpec((1, TILE, d), lambda bb, i: (bb, jnp.maximum(i - 1, 0), 0))
        out_shape = jax.ShapeDtypeStruct((b, lp - TILE, d), F32)
    else:
        out_spec = tile_spec
        out_shape = jax.ShapeDtypeStruct((b, lp, d), F32)
    kern = functools.partial(_attn_layer_kernel, lambda_init=lambda_init, final=final)
    return pl.pallas_call(
        kern,
        grid=(b, nt),
        in_specs=[
            tile_spec,
            _const_spec((1, d)),
            _const_spec(wq.shape),
            _const_spec(wk.shape),
            _const_spec(wvt.shape),
            _const_spec(wz.shape),
            _const_spec((1, HEAD_DIM)),
            _const_spec((1, HEAD_DIM)),
            _const_spec((1, HEAD_DIM)),
            _const_spec((1, HEAD_DIM)),
            _const_spec((1, v_width)),
            _const_spec(w_out.shape),
            _const_spec((1, d)),
        ],
        out_specs=out_spec,
        out_shape=out_shape,
        scratch_shapes=[
            pltpu.VMEM((nt, TILE, qk_width), BF16),
            pltpu.VMEM((nt, v_width, TILE), BF16),
            pltpu.VMEM((HEADS, 2 * TILE, V_DIM + LANES), BF16),
            pltpu.VMEM((TILE, LANES), BF16),
            pltpu.VMEM((TILE, v_width), F32),
            pltpu.VMEM((v_width, TILE), F32),
            pltpu.VMEM((HEADS, 1, 2 * TILE), F32),
            pltpu.VMEM((HEADS, V_DIM + ONES_ROWS, 2 * TILE), F32),
        ],
        compiler_params=pltpu.CompilerParams(
            dimension_semantics=("arbitrary", "arbitrary"), vmem_limit_bytes=VMEM_LIMIT),
        name="attn_layer_final" if final else "attn_layer",
    )(hp, norm_gain.reshape(1, d), wq, wk, wvt, wz, lq1.reshape(1, HEAD_DIM),
      lk1.reshape(1, HEAD_DIM), lq2.reshape(1, HEAD_DIM), lk2.reshape(1, HEAD_DIM),
      jnp.tile(subln_gain, HEADS).reshape(1, v_width), w_out.astype(BF16),
      final_gain.reshape(1, d))


def kernel(x, meta_tokens, norm_gain, final_norm_gain, conv_w_in, conv_dw_kernel, conv_dw_bias, conv_ln_gain, conv_ln_bias, conv_w_out, attn_w_in, attn_lambda_q1, attn_lambda_k1, attn_lambda_q2, attn_lambda_k2, attn_subln_gain, attn_w_out):
    b, seq, d = x.shape
    assert seq % TILE == 0 and meta_tokens.shape[0] == N_META
    depth = norm_gain.shape[0]
    assert depth % 2 == 0, "the last layer must be an attention layer (it applies the final norm)"
    meta = jnp.broadcast_to(meta_tokens[None].astype(x.dtype), (b, N_META, d))
    hp = jnp.concatenate([jnp.zeros((b, FRONT, d), x.dtype), meta, x], axis=1)
    for layer in range(depth):
        j = layer // 2
        if layer % 2 == 0:
            hp = _conv_layer(hp, norm_gain[layer], conv_w_in[j], conv_dw_kernel[j], conv_dw_bias[j],
                             conv_ln_gain[j], conv_ln_bias[j], conv_w_out[j])
        else:
            lambda_init = 0.8 - 0.6 * math.exp(-0.3 * layer)
            hp = _attn_layer(hp, norm_gain[layer], attn_w_in[j], attn_lambda_q1[j],
                             attn_lambda_k1[j], attn_lambda_q2[j], attn_lambda_k2[j],
                             attn_subln_gain[j], attn_w_out[j], final_norm_gain,
                             lambda_init, final=(layer == depth - 1))
    return hp
```

```python
import functools
import math
import struct

import jax
import jax.numpy as jnp
from jax import lax
from jax.experimental import pallas as pl
from jax.experimental.pallas import tpu as pltpu

F32 = jnp.float32
BF16 = jnp.bfloat16

EPS = 1e-6
NEG_BIG = -1e30
LOG2E = math.log2(math.e)
N_META = 16
TILE = 256
FRONT = TILE - N_META
CONV_KERNEL = 31
SUBLANES = 8
LANES = 128
CARRY = 32
CONV_ROWS = 128
HEADS = 8
HEAD_DIM = 64
V_DIM = 2 * HEAD_DIM
ONES_ROWS = 16
PREFETCH = 2
VMEM_LIMIT = 56 * 1024 * 1024
NT_DIMS = (((1,), (1,)), ((), ()))


def _rms(x, gain):
    return x * lax.rsqrt(jnp.mean(x * x, axis=-1, keepdims=True) + EPS) * gain


def _silu(x):
    return x * jax.nn.sigmoid(x)


def _const_spec(shape):
    return pl.BlockSpec(shape, lambda b, i: (0,) * len(shape), pipeline_mode=pl.Buffered(1))


def _conv_layer_kernel(h_ref, ng_ref, win_ref, dwk_ref, dwb_ref, lng_ref, lnb_ref, wout_ref,
                       o_ref, vbuf_ref, cbuf_ref):
    i = pl.program_id(1)
    width = cbuf_ref.shape[1]
    h = h_ref[0]
    hn = _rms(h, ng_ref[...]).astype(BF16)
    u = jnp.dot(hn, win_ref[:, 0:width], preferred_element_type=F32)
    g = jnp.dot(hn, win_ref[:, width:2 * width], preferred_element_type=F32)
    v = u * jax.nn.sigmoid(g)
    row = lax.broadcasted_iota(jnp.int32, (TILE, 1), 0)
    v = jnp.where((i > 0) | (row >= FRONT), v, 0.0)

    @pl.when(i == 0)
    def _():
        vbuf_ref[0:CARRY, :] = jnp.zeros((CARRY, width), F32)
        vbuf_ref[CARRY + TILE:, :] = jnp.zeros((SUBLANES, width), F32)

    vbuf_ref[CARRY:CARRY + TILE, :] = v

    base = CARRY - (CONV_KERNEL - 1)

    def col_body(c, carry):
        cols = pl.ds(pl.multiple_of(c * LANES, LANES), LANES)
        for r in range(TILE // CONV_ROWS):
            acc = jnp.broadcast_to(dwb_ref[:, cols], (CONV_ROWS, LANES))
            for res in range(SUBLANES):
                part = None
                for j in range(CONV_KERNEL):
                    if (base + j) % SUBLANES != res:
                        continue
                    start = r * CONV_ROWS + (base + j) - res
                    term = dwk_ref[j:j + 1, cols] * vbuf_ref[start:start + CONV_ROWS + SUBLANES, cols]
                    part = term if part is None else part + term
                acc = acc + part[res:res + CONV_ROWS, :]
            cbuf_ref[r * CONV_ROWS:(r + 1) * CONV_ROWS, cols] = acc
        return carry

    lax.fori_loop(0, width // LANES, col_body, 0)
    vbuf_ref[0:CARRY, :] = vbuf_ref[TILE:TILE + CARRY, :]

    c = cbuf_ref[...]
    mu = jnp.mean(c, axis=-1, keepdims=True)
    xc = c - mu
    var = jnp.mean(xc * xc, axis=-1, keepdims=True)
    y = _silu(xc * lax.rsqrt(var + EPS) * lng_ref[...] + lnb_ref[...])
    z = jnp.dot(hn, win_ref[:, 2 * width:3 * width], preferred_element_type=F32)
    y = (y * _silu(z)).astype(BF16)
    o_ref[0] = h + jnp.dot(y, wout_ref[...], preferred_element_type=F32)


def _conv_layer(hp, norm_gain, w_in, dw_kernel, dw_bias, ln_gain, ln_bias, w_out):
    b, lp, d = hp.shape
    width = w_out.shape[0]
    nt = lp // TILE
    tile_spec = pl.BlockSpec((1, TILE, d), lambda bb, i: (bb, i, 0))
    return pl.pallas_call(
        _conv_layer_kernel,
        grid=(b, nt),
        in_specs=[
            tile_spec,
            _const_spec((1, d)),
            _const_spec((d, 3 * width)),
            _const_spec((CONV_KERNEL, width)),
            _const_spec((1, width)),
            _const_spec((1, width)),
            _const_spec((1, width)),
            _const_spec((width, d)),
        ],
        out_specs=tile_spec,
        out_shape=jax.ShapeDtypeStruct((b, lp, d), F32),
        scratch_shapes=[
            pltpu.VMEM((CARRY + TILE + SUBLANES, width), F32),
            pltpu.VMEM((TILE, width), F32),
        ],
        compiler_params=pltpu.CompilerParams(
            dimension_semantics=("arbitrary", "arbitrary"), vmem_limit_bytes=VMEM_LIMIT),
        name="conv_layer",
    )(hp, norm_gain.reshape(1, d), w_in.astype(BF16), dw_kernel, dw_bias.reshape(1, width),
      ln_gain.reshape(1, width), ln_bias.reshape(1, width), w_out.astype(BF16))


def _bf16_round(x):
    bits = struct.unpack("<I", struct.pack("<f", x))[0]
    bits = (bits + 0x7FFF + ((bits >> 16) & 1)) & 0xFFFF0000
    return struct.unpack("<f", struct.pack("<I", bits))[0]


def _split_bf16(x, parts=3):
    out = []
    for _ in range(parts):
        out.append(_bf16_round(x))
        x -= out[-1]
    return out


LOG2E_PARTS = _split_bf16(LOG2E)


def _head_slope(hd):
    return sum(LOG2E_PARTS) * 2.0 ** (-8.0 * (hd + 1) / HEADS)


def _attend(j, i, ntiles, masked, kc_ref, vt_ref, q2_ref, kaug_ref, m_ref, acc_ref):
    keys = ntiles * TILE
    if masked:
        krow = lax.broadcasted_iota(jnp.int32, (keys, 2 * TILE), 0)
        qcol = lax.broadcasted_iota(jnp.int32, (keys, 2 * TILE), 1)
        qcol = jnp.where(qcol >= TILE, qcol - TILE, qcol)
        valid = (qcol - krow + (i - j) * TILE >= 0) & (krow + j * TILE >= FRONT)
    tile_gap = ((j - i) * TILE).astype(F32)
    kaug = kaug_ref[jnp.minimum(j, 1), 0:keys, :]
    ones = jnp.ones((ONES_ROWS, keys), BF16)

    def scores(hd):
        kj = kc_ref[pl.ds(j, ntiles), :, hd * V_DIM:(hd + 1) * V_DIM]
        kj = kj.reshape(keys, V_DIM)
        return lax.dot_general(jnp.concatenate([kj, kaug], axis=1), q2_ref[hd], NT_DIMS,
                               preferred_element_type=F32)

    pending = [scores(hd) for hd in range(PREFETCH)]
    for hd in range(HEADS):
        t = pending.pop(0)
        if hd + PREFETCH < HEADS:
            pending.append(scores(hd + PREFETCH))
        if masked:
            t = jnp.where(valid, t, NEG_BIG)
        gap = _head_slope(hd) * tile_gap
        m_prev = m_ref[hd]
        m_new = jnp.maximum(m_prev, jnp.max(t, axis=0, keepdims=True) + gap)
        p = jnp.exp2(t - (m_new - gap)).astype(BF16)
        alpha = jnp.exp2(m_prev - m_new)
        rows = slice(hd * V_DIM, (hd + 1) * V_DIM)
        vt = jnp.concatenate([vt_ref[j + n, rows, :] for n in range(ntiles)], axis=1)
        pv = jnp.dot(jnp.concatenate([vt, ones], axis=0), p,
                     preferred_element_type=F32)
        acc_ref[hd] = alpha * acc_ref[hd] + pv
        m_ref[hd] = m_new


def _attn_layer_kernel(h_ref, ng_ref, wq_ref, wk_ref, wvt_ref, wz_ref, lq1_ref, lk1_ref, lq2_ref,
                       lk2_ref, sg_ref, wout_ref, fg_ref, o_ref, kc_ref, vt_ref, q2_ref, kaug_ref,
                       zs_ref, ogt_ref, m_ref, acc_ref, *, lambda_init, final):
    i = pl.program_id(1)
    h = h_ref[0]
    hn = _rms(h, ng_ref[...]).astype(BF16)
    q = jnp.dot(hn, wq_ref[...], preferred_element_type=F32) * (LOG2E * HEAD_DIM ** -0.5)
    q = q.astype(BF16)
    lane = lax.broadcasted_iota(jnp.int32, (1, V_DIM), 1)
    nparts = len(LOG2E_PARTS)

    @pl.when(i == 0)
    def _():
        krow = lax.broadcasted_iota(jnp.int32, (2 * TILE, LANES), 0)
        lane_k = lax.broadcasted_iota(jnp.int32, (2 * TILE, LANES), 1)
        low = jnp.where(krow >= TILE, krow - TILE, krow).astype(F32)
        high = jnp.where(krow >= TILE, float(TILE), 0.0)
        rows = jnp.where(lane_k < nparts, low, jnp.where(lane_k < 2 * nparts, high, 0.0))
        kaug_ref[1] = rows.astype(BF16)
        penalty = jnp.where((lane_k == 2 * nparts) & (krow < FRONT), -(2.0 ** 100), 0.0)
        kaug_ref[0] = (rows + penalty).astype(BF16)
        for hd in range(HEADS):
            slope = jnp.where(lane == 2 * nparts, 1.0, jnp.zeros((1, LANES), F32))
            for n, part in enumerate(LOG2E_PARTS):
                slope = jnp.where((lane == n) | (lane == n + nparts),
                                  part * 2.0 ** (-8.0 * (hd + 1) / HEADS), slope)
            q2_ref[hd, :, V_DIM:] = jnp.broadcast_to(slope, (2 * TILE, LANES)).astype(BF16)

    for hd in range(HEADS):
        qh = q[:, hd * V_DIM:(hd + 1) * V_DIM]
        q2_ref[hd, 0:TILE, 0:V_DIM] = jnp.where(lane < HEAD_DIM, qh, jnp.zeros_like(qh))
        q2_ref[hd, TILE:, 0:V_DIM] = jnp.where(lane >= HEAD_DIM, qh, jnp.zeros_like(qh))
    kc_ref[i] = jnp.dot(hn, wk_ref[...], preferred_element_type=F32).astype(BF16)
    vt_ref[i] = lax.dot_general(wvt_ref[...], hn, NT_DIMS,
                                preferred_element_type=F32).astype(BF16)
    zs_ref[...] = _silu(jnp.dot(hn, wz_ref[...], preferred_element_type=F32))

    m_ref[...] = jnp.full(m_ref.shape, NEG_BIG, F32)
    acc_ref[...] = jnp.zeros(acc_ref.shape, F32)
    state = (kc_ref, vt_ref, q2_ref, kaug_ref, m_ref, acc_ref)

    _attend(i, i, 1, True, *state)

    def pair_step(t, carry):
        _attend(2 * t, i, 2, False, *state)
        return carry

    lax.fori_loop(0, i // 2, pair_step, 0)

    @pl.when(i % 2 == 1)
    def _():
        _attend(i - 1, i, 1, False, *state)

    lam = (jnp.exp(jnp.sum(lq1_ref[...] * lk1_ref[...], axis=-1, keepdims=True))
           - jnp.exp(jnp.sum(lq2_ref[...] * lk2_ref[...], axis=-1, keepdims=True))
           + lambda_init)
    for hd in range(HEADS):
        acc = acc_ref[hd, 0:V_DIM, :]
        l = acc_ref[hd, V_DIM:V_DIM + 1, :]
        o_t = acc[:, 0:TILE] / l[:, 0:TILE] - lam * (acc[:, TILE:] / l[:, TILE:])
        ogt_ref[hd * V_DIM:(hd + 1) * V_DIM, :] = o_t * lax.rsqrt(
            jnp.mean(o_t * o_t, axis=0, keepdims=True) + EPS)
    og = ogt_ref[...].T * sg_ref[...] * (1.0 - lambda_init) * zs_ref[...]
    y = h + jnp.dot(og.astype(BF16), wout_ref[...], preferred_element_type=F32)
    if final:
        y = _rms(y, fg_ref[...])
    o_ref[0] = y


def _attn_layer(hp, norm_gain, w_in, lq1, lk1, lq2, lk2, subln_gain, w_out, final_gain,
                lambda_init, final):
    b, lp, d = hp.shape
    nt = lp // TILE
    qk_width = HEADS * 2 * HEAD_DIM
    v_width = HEADS * V_DIM
    w_in = w_in.astype(BF16)
    wq = w_in[:, 0:qk_width]
    wk = w_in[:, qk_width:2 * qk_width]
    wvt = w_in[:, 2 * qk_width:2 * qk_width + v_width].T
    wz = w_in[:, 2 * qk_width + v_width:]
    tile_spec = pl.BlockSpec((1, TILE, d), lambda bb, i: (bb, i, 0))
    if final:
        out_spec = pl.BlockSpec((1, TILE, d), lambda bb, i: (bb, jnp.maximum(i - 1, 0), 0))
        out_shape = jax.ShapeDtypeStruct((b, lp - TILE, d), F32)
    else:
        out_spec = tile_spec
        out_shape = jax.ShapeDtypeStruct((b, lp, d), F32)
    kern = functools.partial(_attn_layer_kernel, lambda_init=lambda_init, final=final)
    return pl.pallas_call(
        kern,
        grid=(b, nt),
        in_specs=[
            tile_spec,
            _const_spec((1, d)),
            _const_spec(wq.shape),
            _const_spec(wk.shape),
            _const_spec(wvt.shape),
            _const_spec(wz.shape),
            _const_spec((1, HEAD_DIM)),
            _const_spec((1, HEAD_DIM)),
            _const_spec((1, HEAD_DIM)),
            _const_spec((1, HEAD_DIM)),
            _const_spec((1, v_width)),
            _const_spec(w_out.shape),
            _const_spec((1, d)),
        ],
        out_specs=out_spec,
        out_shape=out_shape,
        scratch_shapes=[
            pltpu.VMEM((nt, TILE, qk_width), BF16),
            pltpu.VMEM((nt, v_width, TILE), BF16),
            pltpu.VMEM((HEADS, 2 * TILE, V_DIM + LANES), BF16),
            pltpu.VMEM((2, 2 * TILE, LANES), BF16),
            pltpu.VMEM((TILE, v_width), F32),
            pltpu.VMEM((v_width, TILE), F32),
            pltpu.VMEM((HEADS, 1, 2 * TILE), F32),
            pltpu.VMEM((HEADS, V_DIM + ONES_ROWS, 2 * TILE), F32),
        ],
        compiler_params=pltpu.CompilerParams(
            dimension_semantics=("arbitrary", "arbitrary"), vmem_limit_bytes=VMEM_LIMIT),
        name="attn_layer_final" if final else "attn_layer",
    )(hp, norm_gain.reshape(1, d), wq, wk, wvt, wz, lq1.reshape(1, HEAD_DIM),
      lk1.reshape(1, HEAD_DIM), lq2.reshape(1, HEAD_DIM), lk2.reshape(1, HEAD_DIM),
      jnp.tile(subln_gain, HEADS).reshape(1, v_width), w_out.astype(BF16),
      final_gain.reshape(1, d))


def kernel(x, meta_tokens, norm_gain, final_norm_gain, conv_w_in, conv_dw_kernel, conv_dw_bias, conv_ln_gain, conv_ln_bias, conv_w_out, attn_w_in, attn_lambda_q1, attn_lambda_k1, attn_lambda_q2, attn_lambda_k2, attn_subln_gain, attn_w_out):
    b, seq, d = x.shape
    assert seq % TILE == 0 and meta_tokens.shape[0] == N_META
    depth = norm_gain.shape[0]
    assert depth % 2 == 0, "the last layer must be an attention layer (it applies the final norm)"
    meta = jnp.broadcast_to(meta_tokens[None].astype(x.dtype), (b, N_META, d))
    hp = jnp.concatenate([jnp.zeros((b, FRONT, d), x.dtype), meta, x], axis=1)
    for layer in range(depth):
        j = layer // 2
        if layer % 2 == 0:
            hp = _conv_layer(hp, norm_gain[layer], conv_w_in[j], conv_dw_kernel[j], conv_dw_bias[j],
                             conv_ln_gain[j], conv_ln_bias[j], conv_w_out[j])
        else:
            lambda_init = 0.8 - 0.6 * math.exp(-0.3 * layer)
            hp = _attn_layer(hp, norm_gain[layer], attn_w_in[j], attn_lambda_q1[j],
                             attn_lambda_k1[j], attn_lambda_q2[j], attn_lambda_k2[j],
                             attn_subln_gain[j], attn_w_out[j], final_norm_gain,
                             lambda_init, final=(layer == depth - 1))
    return hp
```

```python
import functools
import math
import struct

import jax
import jax.numpy as jnp
from jax import lax
from jax.experimental import pallas as pl
from jax.experimental.pallas import tpu as pltpu

F32 = jnp.float32
BF16 = jnp.bfloat16

EPS = 1e-6
NEG_BIG = -1e30
LOG2E = math.log2(math.e)
N_META = 16
TILE = 256
FRONT = TILE - N_META
CONV_KERNEL = 31
SUBLANES = 8
LANES = 128
CARRY = 32
CONV_ROWS = 128
CONV_CHUNK = 256
HEADS = 8
HEAD_DIM = 64
V_DIM = 2 * HEAD_DIM
ONES_ROWS = 16
PREFETCH = 2
VMEM_LIMIT = 56 * 1024 * 1024
NT_DIMS = (((1,), (1,)), ((), ()))


def _rms(x, gain):
    return x * lax.rsqrt(jnp.mean(x * x, axis=-1, keepdims=True) + EPS) * gain


def _silu(x):
    return x * jax.nn.sigmoid(x)


def _const_spec(shape):
    return pl.BlockSpec(shape, lambda b, i: (0,) * len(shape), pipeline_mode=pl.Buffered(1))


def _conv_layer_kernel(h_ref, meta_ref, ng_ref, win_ref, dwk_ref, dwb_ref, lng_ref, lnb_ref,
                       wout_ref, o_ref, vbuf_ref, cbuf_ref, zs_ref, *, first):
    i = pl.program_id(1)
    width = cbuf_ref.shape[1]
    h = h_ref[0]
    if first:
        head = jnp.concatenate([jnp.zeros((FRONT, h.shape[1]), F32), meta_ref[...]], axis=0)
        h = jnp.where(i == 0, head, h)
    hn = _rms(h, ng_ref[...]).astype(BF16)
    row = lax.broadcasted_iota(jnp.int32, (TILE, 1), 0)
    keep = (i > 0) | (row >= FRONT)

    @pl.when(i == 0)
    def _():
        vbuf_ref[0:CARRY, :] = jnp.zeros((CARRY, width), F32)
        vbuf_ref[CARRY + TILE:, :] = jnp.zeros((SUBLANES, width), F32)

    def glu(c):
        cols = slice(c * CONV_CHUNK, (c + 1) * CONV_CHUNK)
        u = jnp.dot(hn, win_ref[:, cols], preferred_element_type=F32)
        g = jnp.dot(hn, win_ref[:, width + c * CONV_CHUNK:width + (c + 1) * CONV_CHUNK],
                    preferred_element_type=F32)
        vbuf_ref[CARRY:CARRY + TILE, cols] = jnp.where(keep, u * jax.nn.sigmoid(g), 0.0)
        z = jnp.dot(hn, win_ref[:, 2 * width + c * CONV_CHUNK:2 * width + (c + 1) * CONV_CHUNK],
                    preferred_element_type=F32)
        zs_ref[:, cols] = _silu(z)

    base = CARRY - (CONV_KERNEL - 1)

    def taps(c):
        for lane_block in range(c * CONV_CHUNK // LANES, (c + 1) * CONV_CHUNK // LANES):
            cols = slice(lane_block * LANES, (lane_block + 1) * LANES)
            for r in range(TILE // CONV_ROWS):
                acc = jnp.broadcast_to(dwb_ref[:, cols], (CONV_ROWS, LANES))
                for res in range(SUBLANES):
                    part = None
                    for j in range(CONV_KERNEL):
                        if (base + j) % SUBLANES != res:
                            continue
                        start = r * CONV_ROWS + (base + j) - res
                        term = dwk_ref[j:j + 1, cols] * vbuf_ref[start:start + CONV_ROWS + SUBLANES, cols]
                        part = term if part is None else part + term
                    acc = acc + part[res:res + CONV_ROWS, :]
                cbuf_ref[r * CONV_ROWS:(r + 1) * CONV_ROWS, cols] = acc
            vbuf_ref[0:CARRY, cols] = vbuf_ref[TILE:TILE + CARRY, cols]

    glu(0)
    for c in range(width // CONV_CHUNK):
        if c + 1 < width // CONV_CHUNK:
            glu(c + 1)
        taps(c)

    c = cbuf_ref[...]
    mu = jnp.mean(c, axis=-1, keepdims=True)
    xc = c - mu
    var = jnp.mean(xc * xc, axis=-1, keepdims=True)
    y = _silu(xc * lax.rsqrt(var + EPS) * lng_ref[...] + lnb_ref[...])
    y = (y * zs_ref[...]).astype(BF16)
    o_ref[0] = h + jnp.dot(y, wout_ref[...], preferred_element_type=F32)


def _conv_layer(hp, meta, norm_gain, w_in, dw_kernel, dw_bias, ln_gain, ln_bias, w_out, first):
    b, lp, d = hp.shape
    if first:
        lp += TILE
    width = w_out.shape[0]
    nt = lp // TILE
    tile_spec = pl.BlockSpec((1, TILE, d), lambda bb, i: (bb, i, 0))
    if first:
        in_tile_spec = pl.BlockSpec((1, TILE, d), lambda bb, i: (bb, jnp.maximum(i - 1, 0), 0))
    else:
        in_tile_spec = tile_spec
    return pl.pallas_call(
        functools.partial(_conv_layer_kernel, first=first),
        grid=(b, nt),
        in_specs=[
            in_tile_spec,
            _const_spec((N_META, d)),
            _const_spec((1, d)),
            _const_spec((d, 3 * width)),
            _const_spec((CONV_KERNEL, width)),
            _const_spec((1, width)),
            _const_spec((1, width)),
            _const_spec((1, width)),
            _const_spec((width, d)),
        ],
        out_specs=tile_spec,
        out_shape=jax.ShapeDtypeStruct((b, lp, d), F32),
        scratch_shapes=[
            pltpu.VMEM((CARRY + TILE + SUBLANES, width), F32),
            pltpu.VMEM((TILE, width), F32),
            pltpu.VMEM((TILE, width), F32),
        ],
        compiler_params=pltpu.CompilerParams(
            dimension_semantics=("arbitrary", "arbitrary"), vmem_limit_bytes=VMEM_LIMIT),
        name="conv_layer",
    )(hp, meta, norm_gain.reshape(1, d), w_in.astype(BF16), dw_kernel, dw_bias.reshape(1, width),
      ln_gain.reshape(1, width), ln_bias.reshape(1, width), w_out.astype(BF16))


def _bf16_round(x):
    bits = struct.unpack("<I", struct.pack("<f", x))[0]
    bits = (bits + 0x7FFF + ((bits >> 16) & 1)) & 0xFFFF0000
    return struct.unpack("<f", struct.pack("<I", bits))[0]


def _split_bf16(x, parts=3):
    out = []
    for _ in range(parts):
        out.append(_bf16_round(x))
        x -= out[-1]
    return out


LOG2E_PARTS = _split_bf16(LOG2E)


def _head_slope(hd):
    return sum(LOG2E_PARTS) * 2.0 ** (-8.0 * (hd + 1) / HEADS)


def _attend(j, i, ntiles, masked, kc_ref, vt_ref, q2_ref, kaug_ref, m_ref, acc_ref):
    keys = ntiles * TILE
    if masked:
        krow = lax.broadcasted_iota(jnp.int32, (keys, 2 * TILE), 0)
        qcol = lax.broadcasted_iota(jnp.int32, (keys, 2 * TILE), 1)
        qcol = jnp.where(qcol >= TILE, qcol - TILE, qcol)
        valid = (qcol - krow + (i - j) * TILE >= 0) & (krow + j * TILE >= FRONT)
    tile_gap = ((j - i) * TILE).astype(F32)
    kaug = kaug_ref[jnp.minimum(j, 1), 0:keys, :]
    ones = jnp.ones((ONES_ROWS, keys), BF16)

    def scores(hd):
        kj = kc_ref[pl.ds(j, ntiles), :, hd * V_DIM:(hd + 1) * V_DIM]
        kj = kj.reshape(keys, V_DIM)
        return lax.dot_general(jnp.concatenate([kj, kaug], axis=1), q2_ref[hd], NT_DIMS,
                               preferred_element_type=F32)

    pending = [scores(hd) for hd in range(PREFETCH)]
    for hd in range(HEADS):
        t = pending.pop(0)
        if hd + PREFETCH < HEADS:
            pending.append(scores(hd + PREFETCH))
        if masked:
            t = jnp.where(valid, t, NEG_BIG)
        gap = _head_slope(hd) * tile_gap
        m_prev = m_ref[hd]
        m_new = jnp.maximum(m_prev, jnp.max(t, axis=0, keepdims=True) + gap)
        p = jnp.exp2(t - (m_new - gap)).astype(BF16)
        alpha = jnp.exp2(m_prev - m_new)
        rows = slice(hd * V_DIM, (hd + 1) * V_DIM)
        vt = jnp.concatenate([vt_ref[j + n, rows, :] for n in range(ntiles)], axis=1)
        pv = jnp.dot(jnp.concatenate([vt, ones], axis=0), p,
                     preferred_element_type=F32)
        acc_ref[hd] = alpha * acc_ref[hd] + pv
        m_ref[hd] = m_new


def _attn_layer_kernel(h_ref, ng_ref, wq_ref, wk_ref, wvt_ref, wz_ref, lq1_ref, lk1_ref, lq2_ref,
                       lk2_ref, sg_ref, wout_ref, fg_ref, o_ref, kc_ref, vt_ref, q2_ref, kaug_ref,
                       zs_ref, ogt_ref, m_ref, acc_ref, *, lambda_init, final):
    i = pl.program_id(1)
    h = h_ref[0]
    hn = _rms(h, ng_ref[...]).astype(BF16)
    q = jnp.dot(hn, wq_ref[...], preferred_element_type=F32) * (LOG2E * HEAD_DIM ** -0.5)
    q = q.astype(BF16)
    lane = lax.broadcasted_iota(jnp.int32, (1, V_DIM), 1)
    nparts = len(LOG2E_PARTS)

    @pl.when(i == 0)
    def _():
        krow = lax.broadcasted_iota(jnp.int32, (2 * TILE, LANES), 0)
        lane_k = lax.broadcasted_iota(jnp.int32, (2 * TILE, LANES), 1)
        low = jnp.where(krow >= TILE, krow - TILE, krow).astype(F32)
        high = jnp.where(krow >= TILE, float(TILE), 0.0)
        rows = jnp.where(lane_k < nparts, low, jnp.where(lane_k < 2 * nparts, high, 0.0))
        kaug_ref[1] = rows.astype(BF16)
        penalty = jnp.where((lane_k == 2 * nparts) & (krow < FRONT), -(2.0 ** 100), 0.0)
        kaug_ref[0] = (rows + penalty).astype(BF16)
        for hd in range(HEADS):
            slope = jnp.where(lane == 2 * nparts, 1.0, jnp.zeros((1, LANES), F32))
            for n, part in enumerate(LOG2E_PARTS):
                slope = jnp.where((lane == n) | (lane == n + nparts),
                                  part * 2.0 ** (-8.0 * (hd + 1) / HEADS), slope)
            q2_ref[hd, :, V_DIM:] = jnp.broadcast_to(slope, (2 * TILE, LANES)).astype(BF16)

    for hd in range(HEADS):
        qh = q[:, hd * V_DIM:(hd + 1) * V_DIM]
        q2_ref[hd, 0:TILE, 0:V_DIM] = jnp.where(lane < HEAD_DIM, qh, jnp.zeros_like(qh))
        q2_ref[hd, TILE:, 0:V_DIM] = jnp.where(lane >= HEAD_DIM, qh, jnp.zeros_like(qh))
    kc_ref[i] = jnp.dot(hn, wk_ref[...], preferred_element_type=F32).astype(BF16)
    vt_ref[i] = lax.dot_general(wvt_ref[...], hn, NT_DIMS,
                                preferred_element_type=F32).astype(BF16)
    zs_ref[...] = _silu(jnp.dot(hn, wz_ref[...], preferred_element_type=F32))

    m_ref[...] = jnp.full(m_ref.shape, NEG_BIG, F32)
    acc_ref[...] = jnp.zeros(acc_ref.shape, F32)
    state = (kc_ref, vt_ref, q2_ref, kaug_ref, m_ref, acc_ref)

    _attend(i, i, 1, True, *state)

    def pair_step(t, carry):
        _attend(2 * t, i, 2, False, *state)
        return carry

    lax.fori_loop(0, i // 2, pair_step, 0)

    @pl.when(i % 2 == 1)
    def _():
        _attend(i - 1, i, 1, False, *state)

    lam = (jnp.exp(jnp.sum(lq1_ref[...] * lk1_ref[...], axis=-1, keepdims=True))
           - jnp.exp(jnp.sum(lq2_ref[...] * lk2_ref[...], axis=-1, keepdims=True))
           + lambda_init)
    for hd in range(HEADS):
        acc = acc_ref[hd, 0:V_DIM, :]
        l = acc_ref[hd, V_DIM:V_DIM + 1, :]
        o_t = acc[:, 0:TILE] / l[:, 0:TILE] - lam * (acc[:, TILE:] / l[:, TILE:])
        ogt_ref[hd * V_DIM:(hd + 1) * V_DIM, :] = o_t * lax.rsqrt(
            jnp.mean(o_t * o_t, axis=0, keepdims=True) + EPS)
    og = ogt_ref[...].T * sg_ref[...] * (1.0 - lambda_init) * zs_ref[...]
    y = h + jnp.dot(og.astype(BF16), wout_ref[...], preferred_element_type=F32)
    if final:
        y = _rms(y, fg_ref[...])
    o_ref[0] = y


def _attn_layer(hp, norm_gain, w_in, lq1, lk1, lq2, lk2, subln_gain, w_out, final_gain,
                lambda_init, final):
    b, lp, d = hp.shape
    nt = lp // TILE
    qk_width = HEADS * 2 * HEAD_DIM
    v_width = HEADS * V_DIM
    w_in = w_in.astype(BF16)
    wq = w_in[:, 0:qk_width]
    wk = w_in[:, qk_width:2 * qk_width]
    wvt = w_in[:, 2 * qk_width:2 * qk_width + v_width].T
    wz = w_in[:, 2 * qk_width + v_width:]
    tile_spec = pl.BlockSpec((1, TILE, d), lambda bb, i: (bb, i, 0))
    if final:
        out_spec = pl.BlockSpec((1, TILE, d), lambda bb, i: (bb, jnp.maximum(i - 1, 0), 0))
        out_shape = jax.ShapeDtypeStruct((b, lp - TILE, d), F32)
    else:
        out_spec = tile_spec
        out_shape = jax.ShapeDtypeStruct((b, lp, d), F32)
    kern = functools.partial(_attn_layer_kernel, lambda_init=lambda_init, final=final)
    return pl.pallas_call(
        kern,
        grid=(b, nt),
        in_specs=[
            tile_spec,
            _const_spec((1, d)),
            _const_spec(wq.shape),
            _const_spec(wk.shape),
            _const_spec(wvt.shape),
            _const_spec(wz.shape),
            _const_spec((1, HEAD_DIM)),
            _const_spec((1, HEAD_DIM)),
            _const_spec((1, HEAD_DIM)),
            _const_spec((1, HEAD_DIM)),
            _const_spec((1, v_width)),
            _const_spec(w_out.shape),
            _const_spec((1, d)),
        ],
        out_specs=out_spec,
        out_shape=out_shape,
        scratch_shapes=[
            pltpu.VMEM((nt, TILE, qk_width), BF16),
            pltpu.VMEM((nt, v_width, TILE), BF16),
            pltpu.VMEM((HEADS, 2 * TILE, V_DIM + LANES), BF16),
            pltpu.VMEM((2, 2 * TILE, LANES), BF16),
            pltpu.VMEM((TILE, v_width), F32),
            pltpu.VMEM((v_width, TILE), F32),
            pltpu.VMEM((HEADS, 1, 2 * TILE), F32),
            pltpu.VMEM((HEADS, V_DIM + ONES_ROWS, 2 * TILE), F32),
        ],
        compiler_params=pltpu.CompilerParams(
            dimension_semantics=("arbitrary", "arbitrary"), vmem_limit_bytes=VMEM_LIMIT),
        name="attn_layer_final" if final else "attn_layer",
    )(hp, norm_gain.reshape(1, d), wq, wk, wvt, wz, lq1.reshape(1, HEAD_DIM),
      lk1.reshape(1, HEAD_DIM), lq2.reshape(1, HEAD_DIM), lk2.reshape(1, HEAD_DIM),
      jnp.tile(subln_gain, HEADS).reshape(1, v_width), w_out.astype(BF16),
      final_gain.reshape(1, d))


def kernel(x, meta_tokens, norm_gain, final_norm_gain, conv_w_in, conv_dw_kernel, conv_dw_bias, conv_ln_gain, conv_ln_bias, conv_w_out, attn_w_in, attn_lambda_q1, attn_lambda_k1, attn_lambda_q2, attn_lambda_k2, attn_subln_gain, attn_w_out):
    b, seq, d = x.shape
    assert seq % TILE == 0 and meta_tokens.shape[0] == N_META
    depth = norm_gain.shape[0]
    assert depth % 2 == 0, "the last layer must be an attention layer (it applies the final norm)"
    meta = meta_tokens.astype(x.dtype)
    hp = x
    for layer in range(depth):
        j = layer // 2
        if layer % 2 == 0:
            hp = _conv_layer(hp, meta, norm_gain[layer], conv_w_in[j], conv_dw_kernel[j],
                             conv_dw_bias[j], conv_ln_gain[j], conv_ln_bias[j], conv_w_out[j],
                             first=(layer == 0))
        else:
            lambda_init = 0.8 - 0.6 * math.exp(-0.3 * layer)
            hp = _attn_layer(hp, norm_gain[layer], attn_w_in[j], attn_lambda_q1[j],
                             attn_lambda_k1[j], attn_lambda_q2[j], attn_lambda_k2[j],
                             attn_subln_gain[j], attn_w_out[j], final_norm_gain,
                             lambda_init, final=(layer == depth - 1))
    return hp
```

```python
import functools
import math
import struct

import jax
import jax.numpy as jnp
from jax import lax
from jax.experimental import pallas as pl
from jax.experimental.pallas import tpu as pltpu

F32 = jnp.float32
BF16 = jnp.bfloat16

EPS = 1e-6
NEG_BIG = -1e30
LOG2E = math.log2(math.e)
N_META = 16
TILE = 256
FRONT = TILE - N_META
CONV_KERNEL = 31
SUBLANES = 8
LANES = 128
CARRY = 32
CONV_ROWS = 128
CONV_CHUNK = 256
HEADS = 8
HEAD_DIM = 64
V_DIM = 2 * HEAD_DIM
ONES_ROWS = 16
PREFETCH = 2
GROUP = 3
VMEM_LIMIT = 56 * 1024 * 1024
NT_DIMS = (((1,), (1,)), ((), ()))


def _rms(x, gain):
    return x * lax.rsqrt(jnp.mean(x * x, axis=-1, keepdims=True) + EPS) * gain


def _silu(x):
    return x * jax.nn.sigmoid(x)


def _const_spec(shape):
    return pl.BlockSpec(shape, lambda b, i: (0,) * len(shape), pipeline_mode=pl.Buffered(1))


def _conv_layer_kernel(h_ref, meta_ref, ng_ref, win_ref, dwk_ref, dwb_ref, lng_ref, lnb_ref,
                       wout_ref, o_ref, vbuf_ref, cbuf_ref, zs_ref, *, first):
    i = pl.program_id(1)
    width = cbuf_ref.shape[1]
    h = h_ref[0]
    if first:
        head = jnp.concatenate([jnp.zeros((FRONT, h.shape[1]), F32), meta_ref[...]], axis=0)
        h = jnp.where(i == 0, head, h)
    hn = _rms(h, ng_ref[...]).astype(BF16)
    row = lax.broadcasted_iota(jnp.int32, (TILE, 1), 0)
    keep = (i > 0) | (row >= FRONT)

    @pl.when(i == 0)
    def _():
        vbuf_ref[0:CARRY, :] = jnp.zeros((CARRY, width), F32)
        vbuf_ref[CARRY + TILE:, :] = jnp.zeros((SUBLANES, width), F32)

    def glu(c):
        cols = slice(c * CONV_CHUNK, (c + 1) * CONV_CHUNK)
        u = jnp.dot(hn, win_ref[:, cols], preferred_element_type=F32)
        g = jnp.dot(hn, win_ref[:, width + c * CONV_CHUNK:width + (c + 1) * CONV_CHUNK],
                    preferred_element_type=F32)
        vbuf_ref[CARRY:CARRY + TILE, cols] = jnp.where(keep, u * jax.nn.sigmoid(g), 0.0)
        z = jnp.dot(hn, win_ref[:, 2 * width + c * CONV_CHUNK:2 * width + (c + 1) * CONV_CHUNK],
                    preferred_element_type=F32)
        zs_ref[:, cols] = _silu(z)

    base = CARRY - (CONV_KERNEL - 1)

    def taps(c):
        for lane_block in range(c * CONV_CHUNK // LANES, (c + 1) * CONV_CHUNK // LANES):
            cols = slice(lane_block * LANES, (lane_block + 1) * LANES)
            for r in range(TILE // CONV_ROWS):
                acc = jnp.broadcast_to(dwb_ref[:, cols], (CONV_ROWS, LANES))
                for res in range(SUBLANES):
                    part = None
                    for j in range(CONV_KERNEL):
                        if (base + j) % SUBLANES != res:
                            continue
                        start = r * CONV_ROWS + (base + j) - res
                        term = dwk_ref[j:j + 1, cols] * vbuf_ref[start:start + CONV_ROWS + SUBLANES, cols]
                        part = term if part is None else part + term
                    acc = acc + part[res:res + CONV_ROWS, :]
                cbuf_ref[r * CONV_ROWS:(r + 1) * CONV_ROWS, cols] = acc
            vbuf_ref[0:CARRY, cols] = vbuf_ref[TILE:TILE + CARRY, cols]

    glu(0)
    for c in range(width // CONV_CHUNK):
        if c + 1 < width // CONV_CHUNK:
            glu(c + 1)
        taps(c)

    c = cbuf_ref[...]
    mu = jnp.mean(c, axis=-1, keepdims=True)
    xc = c - mu
    var = jnp.mean(xc * xc, axis=-1, keepdims=True)
    y = _silu(xc * lax.rsqrt(var + EPS) * lng_ref[...] + lnb_ref[...])
    y = (y * zs_ref[...]).astype(BF16)
    o_ref[0] = h + jnp.dot(y, wout_ref[...], preferred_element_type=F32)


def _conv_layer(hp, meta, norm_gain, w_in, dw_kernel, dw_bias, ln_gain, ln_bias, w_out, first):
    b, lp, d = hp.shape
    if first:
        lp += TILE
    width = w_out.shape[0]
    nt = lp // TILE
    tile_spec = pl.BlockSpec((1, TILE, d), lambda bb, i: (bb, i, 0))
    if first:
        in_tile_spec = pl.BlockSpec((1, TILE, d), lambda bb, i: (bb, jnp.maximum(i - 1, 0), 0))
    else:
        in_tile_spec = tile_spec
    return pl.pallas_call(
        functools.partial(_conv_layer_kernel, first=first),
        grid=(b, nt),
        in_specs=[
            in_tile_spec,
            _const_spec((N_META, d)),
            _const_spec((1, d)),
            _const_spec((d, 3 * width)),
            _const_spec((CONV_KERNEL, width)),
            _const_spec((1, width)),
            _const_spec((1, width)),
            _const_spec((1, width)),
            _const_spec((width, d)),
        ],
        out_specs=tile_spec,
        out_shape=jax.ShapeDtypeStruct((b, lp, d), F32),
        scratch_shapes=[
            pltpu.VMEM((CARRY + TILE + SUBLANES, width), F32),
            pltpu.VMEM((TILE, width), F32),
            pltpu.VMEM((TILE, width), F32),
        ],
        compiler_params=pltpu.CompilerParams(
            dimension_semantics=("arbitrary", "arbitrary"), vmem_limit_bytes=VMEM_LIMIT),
        name="conv_layer",
    )(hp, meta, norm_gain.reshape(1, d), w_in.astype(BF16), dw_kernel, dw_bias.reshape(1, width),
      ln_gain.reshape(1, width), ln_bias.reshape(1, width), w_out.astype(BF16))


def _bf16_round(x):
    bits = struct.unpack("<I", struct.pack("<f", x))[0]
    bits = (bits + 0x7FFF + ((bits >> 16) & 1)) & 0xFFFF0000
    return struct.unpack("<f", struct.pack("<I", bits))[0]


def _split_bf16(x, parts=3):
    out = []
    for _ in range(parts):
        out.append(_bf16_round(x))
        x -= out[-1]
    return out


LOG2E_PARTS = _split_bf16(LOG2E)


def _head_slope(hd):
    return sum(LOG2E_PARTS) * 2.0 ** (-8.0 * (hd + 1) / HEADS)


def _attend(j, i, ntiles, masked, kc_ref, vt_ref, q2_ref, kaug_ref, m_ref, acc_ref):
    keys = ntiles * TILE
    if masked:
        krow = lax.broadcasted_iota(jnp.int32, (keys, 2 * TILE), 0)
        qcol = lax.broadcasted_iota(jnp.int32, (keys, 2 * TILE), 1)
        qcol = jnp.where(qcol >= TILE, qcol - TILE, qcol)
        valid = (qcol - krow + (i - j) * TILE >= 0) & (krow + j * TILE >= FRONT)
    tile_gap = ((j - i) * TILE).astype(F32)
    kaug = kaug_ref[jnp.minimum(j, 1), 0:keys, :]
    ones = jnp.ones((ONES_ROWS, keys), BF16)

    def scores(hd):
        kj = kc_ref[pl.ds(j, ntiles), :, hd * V_DIM:(hd + 1) * V_DIM]
        kj = kj.reshape(keys, V_DIM)
        return lax.dot_general(jnp.concatenate([kj, kaug], axis=1), q2_ref[hd], NT_DIMS,
                               preferred_element_type=F32)

    pending = [scores(hd) for hd in range(PREFETCH)]
    for hd in range(HEADS):
        t = pending.pop(0)
        if hd + PREFETCH < HEADS:
            pending.append(scores(hd + PREFETCH))
        if masked:
            t = jnp.where(valid, t, NEG_BIG)
        gap = _head_slope(hd) * tile_gap
        m_prev = m_ref[hd]
        m_new = jnp.maximum(m_prev, jnp.max(t, axis=0, keepdims=True) + gap)
        p = jnp.exp2(t - (m_new - gap)).astype(BF16)
        alpha = jnp.exp2(m_prev - m_new)
        rows = slice(hd * V_DIM, (hd + 1) * V_DIM)
        vt = jnp.concatenate([vt_ref[j + n, rows, :] for n in range(ntiles)], axis=1)
        pv = jnp.dot(jnp.concatenate([vt, ones], axis=0), p,
                     preferred_element_type=F32)
        acc_ref[hd] = alpha * acc_ref[hd] + pv
        m_ref[hd] = m_new


def _attn_layer_kernel(h_ref, ng_ref, wq_ref, wk_ref, wvt_ref, wz_ref, lq1_ref, lk1_ref, lq2_ref,
                       lk2_ref, sg_ref, wout_ref, fg_ref, o_ref, kc_ref, vt_ref, q2_ref, kaug_ref,
                       zs_ref, ogt_ref, m_ref, acc_ref, *, lambda_init, final):
    i = pl.program_id(1)
    h = h_ref[0]
    hn = _rms(h, ng_ref[...]).astype(BF16)
    q = jnp.dot(hn, wq_ref[...], preferred_element_type=F32) * (LOG2E * HEAD_DIM ** -0.5)
    q = q.astype(BF16)
    lane = lax.broadcasted_iota(jnp.int32, (1, V_DIM), 1)
    nparts = len(LOG2E_PARTS)

    @pl.when(i == 0)
    def _():
        krow = lax.broadcasted_iota(jnp.int32, (GROUP * TILE, LANES), 0)
        lane_k = lax.broadcasted_iota(jnp.int32, (GROUP * TILE, LANES), 1)
        high = jnp.zeros((GROUP * TILE, LANES), F32)
        for n in range(1, GROUP):
            high = jnp.where(krow >= n * TILE, float(n * TILE), high)
        low = krow.astype(F32) - high
        rows = jnp.where(lane_k < nparts, low, jnp.where(lane_k < 2 * nparts, high, 0.0))
        kaug_ref[1] = rows.astype(BF16)
        penalty = jnp.where((lane_k == 2 * nparts) & (krow < FRONT), -(2.0 ** 100), 0.0)
        kaug_ref[0] = (rows + penalty).astype(BF16)
        for hd in range(HEADS):
            slope = jnp.where(lane == 2 * nparts, 1.0, jnp.zeros((1, LANES), F32))
            for n, part in enumerate(LOG2E_PARTS):
                slope = jnp.where((lane == n) | (lane == n + nparts),
                                  part * 2.0 ** (-8.0 * (hd + 1) / HEADS), slope)
            q2_ref[hd, :, V_DIM:] = jnp.broadcast_to(slope, (2 * TILE, LANES)).astype(BF16)

    for hd in range(HEADS):
        qh = q[:, hd * V_DIM:(hd + 1) * V_DIM]
        q2_ref[hd, 0:TILE, 0:V_DIM] = jnp.where(lane < HEAD_DIM, qh, jnp.zeros_like(qh))
        q2_ref[hd, TILE:, 0:V_DIM] = jnp.where(lane >= HEAD_DIM, qh, jnp.zeros_like(qh))
    kc_ref[i] = jnp.dot(hn, wk_ref[...], preferred_element_type=F32).astype(BF16)
    vt_ref[i] = lax.dot_general(wvt_ref[...], hn, NT_DIMS,
                                preferred_element_type=F32).astype(BF16)
    zs_ref[...] = _silu(jnp.dot(hn, wz_ref[...], preferred_element_type=F32))

    m_ref[...] = jnp.full(m_ref.shape, NEG_BIG, F32)
    acc_ref[...] = jnp.zeros(acc_ref.shape, F32)
    state = (kc_ref, vt_ref, q2_ref, kaug_ref, m_ref, acc_ref)

    _attend(i, i, 1, True, *state)

    def group_step(t, carry):
        _attend(GROUP * t, i, GROUP, False, *state)
        return carry

    groups = i // GROUP
    lax.fori_loop(0, groups, group_step, 0)
    for left in range(1, GROUP):
        @pl.when(i - GROUP * groups == left)
        def _(left=left):
            _attend(i - left, i, left, False, *state)

    lam = (jnp.exp(jnp.sum(lq1_ref[...] * lk1_ref[...], axis=-1, keepdims=True))
           - jnp.exp(jnp.sum(lq2_ref[...] * lk2_ref[...], axis=-1, keepdims=True))
           + lambda_init)
    for hd in range(HEADS):
        acc = acc_ref[hd, 0:V_DIM, :]
        l = acc_ref[hd, V_DIM:V_DIM + 1, :]
        o_t = acc[:, 0:TILE] / l[:, 0:TILE] - lam * (acc[:, TILE:] / l[:, TILE:])
        ogt_ref[hd * V_DIM:(hd + 1) * V_DIM, :] = o_t * lax.rsqrt(
            jnp.mean(o_t * o_t, axis=0, keepdims=True) + EPS)
    og = ogt_ref[...].T * sg_ref[...] * (1.0 - lambda_init) * zs_ref[...]
    y = h + jnp.dot(og.astype(BF16), wout_ref[...], preferred_element_type=F32)
    if final:
        y = _rms(y, fg_ref[...])
    o_ref[0] = y


def _attn_layer(hp, norm_gain, w_in, lq1, lk1, lq2, lk2, subln_gain, w_out, final_gain,
                lambda_init, final):
    b, lp, d = hp.shape
    nt = lp // TILE
    qk_width = HEADS * 2 * HEAD_DIM
    v_width = HEADS * V_DIM
    w_in = w_in.astype(BF16)
    wq = w_in[:, 0:qk_width]
    wk = w_in[:, qk_width:2 * qk_width]
    wvt = w_in[:, 2 * qk_width:2 * qk_width + v_width].T
    wz = w_in[:, 2 * qk_width + v_width:]
    tile_spec = pl.BlockSpec((1, TILE, d), lambda bb, i: (bb, i, 0))
    if final:
        out_spec = pl.BlockSpec((1, TILE, d), lambda bb, i: (bb, jnp.maximum(i - 1, 0), 0))
        out_shape = jax.ShapeDtypeStruct((b, lp - TILE, d), F32)
    else:
        out_spec = tile_spec
        out_shape = jax.ShapeDtypeStruct((b, lp, d), F32)
    kern = functools.partial(_attn_layer_kernel, lambda_init=lambda_init, final=final)
    return pl.pallas_call(
        kern,
        grid=(b, nt),
        in_specs=[
            tile_spec,
            _const_spec((1, d)),
            _const_spec(wq.shape),
            _const_spec(wk.shape),
            _const_spec(wvt.shape),
            _const_spec(wz.shape),
            _const_spec((1, HEAD_DIM)),
            _const_spec((1, HEAD_DIM)),
            _const_spec((1, HEAD_DIM)),
            _const_spec((1, HEAD_DIM)),
            _const_spec((1, v_width)),
            _const_spec(w_out.shape),
            _const_spec((1, d)),
        ],
        out_specs=out_spec,
        out_shape=out_shape,
        scratch_shapes=[
            pltpu.VMEM((nt, TILE, qk_width), BF16),
            pltpu.VMEM((nt, v_width, TILE), BF16),
            pltpu.VMEM((HEADS, 2 * TILE, V_DIM + LANES), BF16),
            pltpu.VMEM((2, GROUP * TILE, LANES), BF16),
            pltpu.VMEM((TILE, v_width), F32),
            pltpu.VMEM((v_width, TILE), F32),
            pltpu.VMEM((HEADS, 1, 2 * TILE), F32),
            pltpu.VMEM((HEADS, V_DIM + ONES_ROWS, 2 * TILE), F32),
        ],
        compiler_params=pltpu.CompilerParams(
            dimension_semantics=("arbitrary", "arbitrary"), vmem_limit_bytes=VMEM_LIMIT),
        name="attn_layer_final" if final else "attn_layer",
    )(hp, norm_gain.reshape(1, d), wq, wk, wvt, wz, lq1.reshape(1, HEAD_DIM),
      lk1.reshape(1, HEAD_DIM), lq2.reshape(1, HEAD_DIM), lk2.reshape(1, HEAD_DIM),
      jnp.tile(subln_gain, HEADS).reshape(1, v_width), w_out.astype(BF16),
      final_gain.reshape(1, d))


def kernel(x, meta_tokens, norm_gain, final_norm_gain, conv_w_in, conv_dw_kernel, conv_dw_bias, conv_ln_gain, conv_ln_bias, conv_w_out, attn_w_in, attn_lambda_q1, attn_lambda_k1, attn_lambda_q2, attn_lambda_k2, attn_subln_gain, attn_w_out):
    b, seq, d = x.shape
    assert seq % TILE == 0 and meta_tokens.shape[0] == N_META
    depth = norm_gain.shape[0]
    assert depth % 2 == 0, "the last layer must be an attention layer (it applies the final norm)"
    meta = meta_tokens.astype(x.dtype)
    hp = x
    for layer in range(depth):
        j = layer // 2
        if layer % 2 == 0:
            hp = _conv_layer(hp, meta, norm_gain[layer], conv_w_in[j], conv_dw_kernel[j],
                             conv_dw_bias[j], conv_ln_gain[j], conv_ln_bias[j], conv_w_out[j],
                             first=(layer == 0))
        else:
            lambda_init = 0.8 - 0.6 * math.exp(-0.3 * layer)
            hp = _attn_layer(hp, norm_gain[layer], attn_w_in[j], attn_lambda_q1[j],
                             attn_lambda_k1[j], attn_lambda_q2[j], attn_lambda_k2[j],
                             attn_subln_gain[j], attn_w_out[j], final_norm_gain,
                             lambda_init, final=(layer == depth - 1))
    return hp
```

```python
import functools
import math
import struct

import jax
import jax.numpy as jnp
from jax import lax
from jax.experimental import pallas as pl
from jax.experimental.pallas import tpu as pltpu

F32 = jnp.float32
BF16 = jnp.bfloat16

EPS = 1e-6
NEG_BIG = -1e30
LOG2E = math.log2(math.e)
N_META = 16
TILE = 256
FRONT = TILE - N_META
CONV_KERNEL = 31
SUBLANES = 8
LANES = 128
CARRY = 32
CONV_ROWS = 128
CONV_CHUNK = 256
HEADS = 8
HEAD_DIM = 64
V_DIM = 2 * HEAD_DIM
ONES_ROWS = 16
PREFETCH = 2
GROUP = 4
VMEM_LIMIT = 56 * 1024 * 1024
NT_DIMS = (((1,), (1,)), ((), ()))


def _rms(x, gain):
    return x * lax.rsqrt(jnp.mean(x * x, axis=-1, keepdims=True) + EPS) * gain


def _silu(x):
    return x * jax.nn.sigmoid(x)


def _const_spec(shape):
    return pl.BlockSpec(shape, lambda b, i: (0,) * len(shape), pipeline_mode=pl.Buffered(1))


def _conv_layer_kernel(h_ref, meta_ref, ng_ref, win_ref, dwk_ref, dwb_ref, lng_ref, lnb_ref,
                       wout_ref, o_ref, vbuf_ref, cbuf_ref, zs_ref, *, first):
    i = pl.program_id(1)
    width = cbuf_ref.shape[1]
    h = h_ref[0]
    if first:
        head = jnp.concatenate([jnp.zeros((FRONT, h.shape[1]), F32), meta_ref[...]], axis=0)
        h = jnp.where(i == 0, head, h)
    hn = _rms(h, ng_ref[...]).astype(BF16)
    row = lax.broadcasted_iota(jnp.int32, (TILE, 1), 0)
    keep = (i > 0) | (row >= FRONT)

    @pl.when(i == 0)
    def _():
        vbuf_ref[0:CARRY, :] = jnp.zeros((CARRY, width), F32)
        vbuf_ref[CARRY + TILE:, :] = jnp.zeros((SUBLANES, width), F32)

    def glu(c):
        cols = slice(c * CONV_CHUNK, (c + 1) * CONV_CHUNK)
        u = jnp.dot(hn, win_ref[:, cols], preferred_element_type=F32)
        g = jnp.dot(hn, win_ref[:, width + c * CONV_CHUNK:width + (c + 1) * CONV_CHUNK],
                    preferred_element_type=F32)
        vbuf_ref[CARRY:CARRY + TILE, cols] = jnp.where(keep, u * jax.nn.sigmoid(g), 0.0)
        z = jnp.dot(hn, win_ref[:, 2 * width + c * CONV_CHUNK:2 * width + (c + 1) * CONV_CHUNK],
                    preferred_element_type=F32)
        zs_ref[:, cols] = _silu(z)

    base = CARRY - (CONV_KERNEL - 1)

    def taps(c):
        for lane_block in range(c * CONV_CHUNK // LANES, (c + 1) * CONV_CHUNK // LANES):
            cols = slice(lane_block * LANES, (lane_block + 1) * LANES)
            for r in range(TILE // CONV_ROWS):
                acc = jnp.broadcast_to(dwb_ref[:, cols], (CONV_ROWS, LANES))
                for res in range(SUBLANES):
                    part = None
                    for j in range(CONV_KERNEL):
                        if (base + j) % SUBLANES != res:
                            continue
                        start = r * CONV_ROWS + (base + j) - res
                        term = dwk_ref[j:j + 1, cols] * vbuf_ref[start:start + CONV_ROWS + SUBLANES, cols]
                        part = term if part is None else part + term
                    acc = acc + part[res:res + CONV_ROWS, :]
                cbuf_ref[r * CONV_ROWS:(r + 1) * CONV_ROWS, cols] = acc
            vbuf_ref[0:CARRY, cols] = vbuf_ref[TILE:TILE + CARRY, cols]

    glu(0)
    for c in range(width // CONV_CHUNK):
        if c + 1 < width // CONV_CHUNK:
            glu(c + 1)
        taps(c)

    c = cbuf_ref[...]
    mu = jnp.mean(c, axis=-1, keepdims=True)
    xc = c - mu
    var = jnp.mean(xc * xc, axis=-1, keepdims=True)
    y = _silu(xc * lax.rsqrt(var + EPS) * lng_ref[...] + lnb_ref[...])
    y = (y * zs_ref[...]).astype(BF16)
    o_ref[0] = h + jnp.dot(y, wout_ref[...], preferred_element_type=F32)


def _conv_layer(hp, meta, norm_gain, w_in, dw_kernel, dw_bias, ln_gain, ln_bias, w_out, first):
    b, lp, d = hp.shape
    if first:
        lp += TILE
    width = w_out.shape[0]
    nt = lp // TILE
    tile_spec = pl.BlockSpec((1, TILE, d), lambda bb, i: (bb, i, 0))
    if first:
        in_tile_spec = pl.BlockSpec((1, TILE, d), lambda bb, i: (bb, jnp.maximum(i - 1, 0), 0))
    else:
        in_tile_spec = tile_spec
    return pl.pallas_call(
        functools.partial(_conv_layer_kernel, first=first),
        grid=(b, nt),
        in_specs=[
            in_tile_spec,
            _const_spec((N_META, d)),
            _const_spec((1, d)),
            _const_spec((d, 3 * width)),
            _const_spec((CONV_KERNEL, width)),
            _const_spec((1, width)),
            _const_spec((1, width)),
            _const_spec((1, width)),
            _const_spec((width, d)),
        ],
        out_specs=tile_spec,
        out_shape=jax.ShapeDtypeStruct((b, lp, d), F32),
        scratch_shapes=[
            pltpu.VMEM((CARRY + TILE + SUBLANES, width), F32),
            pltpu.VMEM((TILE, width), F32),
            pltpu.VMEM((TILE, width), F32),
        ],
        compiler_params=pltpu.CompilerParams(
            dimension_semantics=("arbitrary", "arbitrary"), vmem_limit_bytes=VMEM_LIMIT),
        name="conv_layer",
    )(hp, meta, norm_gain.reshape(1, d), w_in.astype(BF16), dw_kernel, dw_bias.reshape(1, width),
      ln_gain.reshape(1, width), ln_bias.reshape(1, width), w_out.astype(BF16))


def _bf16_round(x):
    bits = struct.unpack("<I", struct.pack("<f", x))[0]
    bits = (bits + 0x7FFF + ((bits >> 16) & 1)) & 0xFFFF0000
    return struct.unpack("<f", struct.pack("<I", bits))[0]


def _split_bf16(x, parts=3):
    out = []
    for _ in range(parts):
        out.append(_bf16_round(x))
        x -= out[-1]
    return out


LOG2E_PARTS = _split_bf16(LOG2E)


def _head_slope(hd):
    return sum(LOG2E_PARTS) * 2.0 ** (-8.0 * (hd + 1) / HEADS)


def _attend(j, i, ntiles, masked, kc_ref, vt_ref, q2_ref, kaug_ref, m_ref, acc_ref):
    keys = ntiles * TILE
    if masked:
        krow = lax.broadcasted_iota(jnp.int32, (keys, 2 * TILE), 0)
        qcol = lax.broadcasted_iota(jnp.int32, (keys, 2 * TILE), 1)
        qcol = jnp.where(qcol >= TILE, qcol - TILE, qcol)
        valid = (qcol - krow + (i - j) * TILE >= 0) & (krow + j * TILE >= FRONT)
    tile_gap = ((j - i) * TILE).astype(F32)
    kaug = kaug_ref[jnp.minimum(j, 1), 0:keys, :]
    ones = jnp.ones((ONES_ROWS, keys), BF16)

    def scores(hd):
        kj = kc_ref[pl.ds(j, ntiles), :, hd * V_DIM:(hd + 1) * V_DIM]
        kj = kj.reshape(keys, V_DIM)
        return lax.dot_general(jnp.concatenate([kj, kaug], axis=1), q2_ref[hd], NT_DIMS,
                               preferred_element_type=F32)

    pending = [scores(hd) for hd in range(PREFETCH)]
    for hd in range(HEADS):
        t = pending.pop(0)
        if hd + PREFETCH < HEADS:
            pending.append(scores(hd + PREFETCH))
        if masked:
            t = jnp.where(valid, t, NEG_BIG)
        gap = _head_slope(hd) * tile_gap
        m_prev = m_ref[hd]
        m_new = jnp.maximum(m_prev, jnp.max(t, axis=0, keepdims=True) + gap)
        p = jnp.exp2(t - (m_new - gap)).astype(BF16)
        alpha = jnp.exp2(m_prev - m_new)
        rows = slice(hd * V_DIM, (hd + 1) * V_DIM)
        vt = jnp.concatenate([vt_ref[j + n, rows, :] for n in range(ntiles)], axis=1)
        pv = jnp.dot(jnp.concatenate([vt, ones], axis=0), p,
                     preferred_element_type=F32)
        acc_ref[hd] = alpha * acc_ref[hd] + pv
        m_ref[hd] = m_new


def _attn_layer_kernel(h_ref, ng_ref, wq_ref, wk_ref, wvt_ref, wz_ref, lq1_ref, lk1_ref, lq2_ref,
                       lk2_ref, sg_ref, wout_ref, fg_ref, o_ref, kc_ref, vt_ref, q2_ref, kaug_ref,
                       zs_ref, ogt_ref, m_ref, acc_ref, *, lambda_init, final):
    i = pl.program_id(1)
    h = h_ref[0]
    hn = _rms(h, ng_ref[...]).astype(BF16)
    q = jnp.dot(hn, wq_ref[...], preferred_element_type=F32) * (LOG2E * HEAD_DIM ** -0.5)
    q = q.astype(BF16)
    lane = lax.broadcasted_iota(jnp.int32, (1, V_DIM), 1)
    nparts = len(LOG2E_PARTS)

    @pl.when(i == 0)
    def _():
        krow = lax.broadcasted_iota(jnp.int32, (GROUP * TILE, LANES), 0)
        lane_k = lax.broadcasted_iota(jnp.int32, (GROUP * TILE, LANES), 1)
        high = jnp.zeros((GROUP * TILE, LANES), F32)
        for n in range(1, GROUP):
            high = jnp.where(krow >= n * TILE, float(n * TILE), high)
        low = krow.astype(F32) - high
        rows = jnp.where(lane_k < nparts, low, jnp.where(lane_k < 2 * nparts, high, 0.0))
        kaug_ref[1] = rows.astype(BF16)
        penalty = jnp.where((lane_k == 2 * nparts) & (krow < FRONT), -(2.0 ** 100), 0.0)
        kaug_ref[0] = (rows + penalty).astype(BF16)
        for hd in range(HEADS):
            slope = jnp.where(lane == 2 * nparts, 1.0, jnp.zeros((1, LANES), F32))
            for n, part in enumerate(LOG2E_PARTS):
                slope = jnp.where((lane == n) | (lane == n + nparts),
                                  part * 2.0 ** (-8.0 * (hd + 1) / HEADS), slope)
            q2_ref[hd, :, V_DIM:] = jnp.broadcast_to(slope, (2 * TILE, LANES)).astype(BF16)

    for hd in range(HEADS):
        qh = q[:, hd * V_DIM:(hd + 1) * V_DIM]
        q2_ref[hd, 0:TILE, 0:V_DIM] = jnp.where(lane < HEAD_DIM, qh, jnp.zeros_like(qh))
        q2_ref[hd, TILE:, 0:V_DIM] = jnp.where(lane >= HEAD_DIM, qh, jnp.zeros_like(qh))
    kc_ref[i] = jnp.dot(hn, wk_ref[...], preferred_element_type=F32).astype(BF16)
    vt_ref[i] = lax.dot_general(wvt_ref[...], hn, NT_DIMS,
                                preferred_element_type=F32).astype(BF16)
    zs_ref[...] = _silu(jnp.dot(hn, wz_ref[...], preferred_element_type=F32))

    m_ref[...] = jnp.full(m_ref.shape, NEG_BIG, F32)
    acc_ref[...] = jnp.zeros(acc_ref.shape, F32)
    state = (kc_ref, vt_ref, q2_ref, kaug_ref, m_ref, acc_ref)

    _attend(i, i, 1, True, *state)

    def group_step(t, carry):
        _attend(GROUP * t, i, GROUP, False, *state)
        return carry

    groups = i // GROUP
    lax.fori_loop(0, groups, group_step, 0)
    for left in range(1, GROUP):
        @pl.when(i - GROUP * groups == left)
        def _(left=left):
            _attend(i - left, i, left, False, *state)

    lam = (jnp.exp(jnp.sum(lq1_ref[...] * lk1_ref[...], axis=-1, keepdims=True))
           - jnp.exp(jnp.sum(lq2_ref[...] * lk2_ref[...], axis=-1, keepdims=True))
           + lambda_init)
    for hd in range(HEADS):
        acc = acc_ref[hd, 0:V_DIM, :]
        l = acc_ref[hd, V_DIM:V_DIM + 1, :]
        o_t = acc[:, 0:TILE] / l[:, 0:TILE] - lam * (acc[:, TILE:] / l[:, TILE:])
        ogt_ref[hd * V_DIM:(hd + 1) * V_DIM, :] = o_t * lax.rsqrt(
            jnp.mean(o_t * o_t, axis=0, keepdims=True) + EPS)
    og = ogt_ref[...].T * sg_ref[...] * (1.0 - lambda_init) * zs_ref[...]
    y = h + jnp.dot(og.astype(BF16), wout_ref[...], preferred_element_type=F32)
    if final:
        y = _rms(y, fg_ref[...])
    o_ref[0] = y


def _attn_layer(hp, norm_gain, w_in, lq1, lk1, lq2, lk2, subln_gain, w_out, final_gain,
                lambda_init, final):
    b, lp, d = hp.shape
    nt = lp // TILE
    qk_width = HEADS * 2 * HEAD_DIM
    v_width = HEADS * V_DIM
    w_in = w_in.astype(BF16)
    wq = w_in[:, 0:qk_width]
    wk = w_in[:, qk_width:2 * qk_width]
    wvt = w_in[:, 2 * qk_width:2 * qk_width + v_width].T
    wz = w_in[:, 2 * qk_width + v_width:]
    tile_spec = pl.BlockSpec((1, TILE, d), lambda bb, i: (bb, i, 0))
    if final:
        out_spec = pl.BlockSpec((1, TILE, d), lambda bb, i: (bb, jnp.maximum(i - 1, 0), 0))
        out_shape = jax.ShapeDtypeStruct((b, lp - TILE, d), F32)
    else:
        out_spec = tile_spec
        out_shape = jax.ShapeDtypeStruct((b, lp, d), F32)
    kern = functools.partial(_attn_layer_kernel, lambda_init=lambda_init, final=final)
    return pl.pallas_call(
        kern,
        grid=(b, nt),
        in_specs=[
            tile_spec,
            _const_spec((1, d)),
            _const_spec(wq.shape),
            _const_spec(wk.shape),
            _const_spec(wvt.shape),
            _const_spec(wz.shape),
            _const_spec((1, HEAD_DIM)),
            _const_spec((1, HEAD_DIM)),
            _const_spec((1, HEAD_DIM)),
            _const_spec((1, HEAD_DIM)),
            _const_spec((1, v_width)),
            _const_spec(w_out.shape),
            _const_spec((1, d)),
        ],
        out_specs=out_spec,
        out_shape=out_shape,
        scratch_shapes=[
            pltpu.VMEM((nt, TILE, qk_width), BF16),
            pltpu.VMEM((nt, v_width, TILE), BF16),
            pltpu.VMEM((HEADS, 2 * TILE, V_DIM + LANES), BF16),
            pltpu.VMEM((2, GROUP * TILE, LANES), BF16),
            pltpu.VMEM((TILE, v_width), F32),
            pltpu.VMEM((v_width, TILE), F32),
            pltpu.VMEM((HEADS, 1, 2 * TILE), F32),
            pltpu.VMEM((HEADS, V_DIM + ONES_ROWS, 2 * TILE), F32),
        ],
        compiler_params=pltpu.CompilerParams(
            dimension_semantics=("arbitrary", "arbitrary"), vmem_limit_bytes=VMEM_LIMIT),
        name="attn_layer_final" if final else "attn_layer",
    )(hp, norm_gain.reshape(1, d), wq, wk, wvt, wz, lq1.reshape(1, HEAD_DIM),
      lk1.reshape(1, HEAD_DIM), lq2.reshape(1, HEAD_DIM), lk2.reshape(1, HEAD_DIM),
      jnp.tile(subln_gain, HEADS).reshape(1, v_width), w_out.astype(BF16),
      final_gain.reshape(1, d))


def kernel(x, meta_tokens, norm_gain, final_norm_gain, conv_w_in, conv_dw_kernel, conv_dw_bias, conv_ln_gain, conv_ln_bias, conv_w_out, attn_w_in, attn_lambda_q1, attn_lambda_k1, attn_lambda_q2, attn_lambda_k2, attn_subln_gain, attn_w_out):
    b, seq, d = x.shape
    assert seq % TILE == 0 and meta_tokens.shape[0] == N_META
    depth = norm_gain.shape[0]
    assert depth % 2 == 0, "the last layer must be an attention layer (it applies the final norm)"
    meta = meta_tokens.astype(x.dtype)
    hp = x
    for layer in range(depth):
        j = layer // 2
        if layer % 2 == 0:
            hp = _conv_layer(hp, meta, norm_gain[layer], conv_w_in[j], conv_dw_kernel[j],
                             conv_dw_bias[j], conv_ln_gain[j], conv_ln_bias[j], conv_w_out[j],
                             first=(layer == 0))
        else:
            lambda_init = 0.8 - 0.6 * math.exp(-0.3 * layer)
            hp = _attn_layer(hp, norm_gain[layer], attn_w_in[j], attn_lambda_q1[j],
                             attn_lambda_k1[j], attn_lambda_q2[j], attn_lambda_k2[j],
                             attn_subln_gain[j], attn_w_out[j], final_norm_gain,
                             lambda_init, final=(layer == depth - 1))
    return hp
```
